```python
import math
import jax, jax.numpy as jnp
from jax import lax
import numpy as np

D_MODEL = 1024
BATCH = 16
SEQ = 2048
DEPTH = 1
DEC_BATCH = 16
DEC_SEQ = 4096
PAST_LEN = 128

MIX_W = D_MODEL
MLA_HEADS = 8
MLA_NOPE = 64
MLA_ROPE = 32
MLA_V = 64
Q_LORA = 384
KV_LORA = 256
MLA_THETA = 10000.0
MLA_OUT = MLA_HEADS * MLA_V
DIFF_HEADS = 4
DIFF_HD = 64
DIFF_ROT = DIFF_HD // 4
DIFF_THETA = 500000.0
DIFF_QK = DIFF_HEADS * 2 * DIFF_HD
DIFF_VW = DIFF_HEADS * 2 * DIFF_HD
DIFF_OUT = DIFF_VW
P_IN = Q_LORA + KV_LORA + MLA_ROPE + 2 * DIFF_QK + DIFF_VW
N_EXPERTS = 32
TOP_K = 4
D_FF = 1024
SWIGLU_ALPHA = 1.702
SWIGLU_LIMIT = 7.0
RMS_EPS = 1e-5
Q_BLOCK = 128
MOE_BLOCK = 256

kernel_name = "hymba_mla_diffattn_moe_encoder"


def rmsnorm(x, g):
    xf = x.astype(jnp.float32)
    y = xf * lax.rsqrt(jnp.mean(xf * xf, axis=-1, keepdims=True) + RMS_EPS)
    return (y * g.astype(jnp.float32)).astype(x.dtype)


def apply_rope(x, rot_dim, theta):
    S = x.shape[1]
    half = rot_dim // 2
    inv_freq = jnp.power(theta, -2.0 * jnp.arange(half, dtype=jnp.float32) / rot_dim)
    ang = jnp.arange(S, dtype=jnp.float32)[:, None] * inv_freq[None, :]
    ang = ang.reshape((1, S) + (1,) * (x.ndim - 3) + (half,))
    cos = jnp.cos(ang).astype(x.dtype)
    sin = jnp.sin(ang).astype(x.dtype)
    x1 = x[..., :half]
    x2 = x[..., half:rot_dim]
    rest = x[..., rot_dim:]
    return jnp.concatenate([x1 * cos - x2 * sin, x2 * cos + x1 * sin, rest], axis=-1)


def block_attention(q, k, v, scale):
    B, S, H, M, Dk = q.shape
    nb = S // Q_BLOCK
    qb = q.reshape(B, nb, Q_BLOCK, H, M, Dk).transpose(1, 0, 2, 3, 4, 5)

    def one_block(q_blk):
        s = jnp.einsum('bqhmd,bkhmd->bhmqk', q_blk, k, preferred_element_type=jnp.float32) * scale
        p = jax.nn.softmax(s, axis=-1).astype(v.dtype)
        return jnp.einsum('bhmqk,bkhe->bqhme', p, v)

    out = lax.map(one_block, qb)
    return out.transpose(1, 0, 2, 3, 4, 5).reshape(B, S, H, M, v.shape[-1])


def moe(h, w_router, b_router, w_gate, b_gate, w_up, b_up, w_down, b_down):
    T, D = h.shape
    logits = jnp.einsum('td,de->te', h, w_router, preferred_element_type=jnp.float32) + b_router.astype(jnp.float32)
    top_vals, top_idx = lax.top_k(logits, TOP_K)
    gates = jax.nn.softmax(top_vals, axis=-1)
    n_assign = T * TOP_K
    flat_e = top_idx.reshape(-1)
    order = jnp.argsort(flat_e, stable=True)
    sorted_e = flat_e[order]
    sorted_tok = (order // TOP_K).astype(jnp.int32)
    sorted_gate = gates.reshape(-1)[order]
    counts = jnp.bincount(flat_e, length=N_EXPERTS)
    padded = ((counts + MOE_BLOCK - 1) // MOE_BLOCK) * MOE_BLOCK
    pad_end = jnp.cumsum(padded)
    pad_start = pad_end - padded
    start = jnp.cumsum(counts) - counts
    dest = pad_start[sorted_e] + (jnp.arange(n_assign, dtype=jnp.int32) - start[sorted_e])
    n_blocks = -(-n_assign // MOE_BLOCK) + N_EXPERTS
    n_slots = n_blocks * MOE_BLOCK
    slot_tok = jnp.zeros((n_slots,), jnp.int32).at[dest].set(sorted_tok)
    slot_gate = jnp.zeros((n_slots,), jnp.float32).at[dest].set(sorted_gate)
    block_e = jnp.minimum(
        jnp.searchsorted(pad_end, jnp.arange(n_blocks, dtype=jnp.int32) * MOE_BLOCK, side='right'),
        N_EXPERTS - 1)
    xs = h[slot_tok].reshape(n_blocks, MOE_BLOCK, D)

    def expert_block(args):
        xb, e = args
        g = xb @ w_gate[e] + b_gate[e]
        u = xb @ w_up[e] + b_up[e]
        g = jnp.minimum(g, SWIGLU_LIMIT)
        u = jnp.clip(u, -SWIGLU_LIMIT, SWIGLU_LIMIT)
        a = g * jax.nn.sigmoid(SWIGLU_ALPHA * g) * (u + 1)
        return a @ w_down[e] + b_down[e]

    ys = lax.map(expert_block, (xs, block_e)).reshape(n_slots, D)
    return jnp.zeros((T, D), h.dtype).at[slot_tok].add(ys * slot_gate[:, None].astype(ys.dtype))


def encoder_layer(x, layer_idx, g_attn, w_in, g_cq, w_uq, g_ckv, w_uk, w_uv,
                  lam_q1, lam_k1, lam_q2, lam_k2, g_subln, w_out, g_ffn,
                  w_router, b_router, w_gate, b_gate, w_up, b_up, w_down, b_down):
    B, S, D = x.shape
    h = rmsnorm(x, g_attn)
    proj = h @ w_in
    o1 = Q_LORA
    o2 = o1 + KV_LORA
    o3 = o2 + MLA_ROPE
    o4 = o3 + DIFF_QK
    o5 = o4 + DIFF_QK
    c_q, c_kv, k_pe, dq, dk, dv = jnp.split(proj, [o1, o2, o3, o4, o5], axis=-1)

    q = (rmsnorm(c_q, g_cq) @ w_uq).reshape(B, S, MLA_HEADS, MLA_NOPE + MLA_ROPE)
    q = jnp.concatenate([q[..., :MLA_NOPE], apply_rope(q[..., MLA_NOPE:], MLA_ROPE, MLA_THETA)], axis=-1)
    c_kv_n = rmsnorm(c_kv, g_ckv)
    k_nope = (c_kv_n @ w_uk).reshape(B, S, MLA_HEADS, MLA_NOPE)
    v_mla = (c_kv_n @ w_uv).reshape(B, S, MLA_HEADS, MLA_V)
    k_pe = apply_rope(k_pe[:, :, None, :], MLA_ROPE, MLA_THETA)
    k = jnp.concatenate([k_nope, jnp.broadcast_to(k_pe, (B, S, MLA_HEADS, MLA_ROPE))], axis=-1)
    o_mla = block_attention(q[:, :, :, None], k[:, :, :, None], v_mla,
                            1.0 / math.sqrt(MLA_NOPE + MLA_ROPE))[:, :, :, 0]
    o_mla = o_mla.reshape(B, S, MLA_OUT)

    dq = apply_rope(dq.reshape(B, S, DIFF_HEADS, 2, DIFF_HD), DIFF_ROT, DIFF_THETA)
    dk = apply_rope(dk.reshape(B, S, DIFF_HEADS, 2, DIFF_HD), DIFF_ROT, DIFF_THETA)
    dv = dv.reshape(B, S, DIFF_HEADS, 2 * DIFF_HD)
    od = block_attention(dq, dk, dv, 1.0 / math.sqrt(DIFF_HD))
    lambda_init = 0.8 - 0.6 * math.exp(-0.3 * layer_idx)
    lam = (jnp.exp(jnp.sum(lam_q1.astype(jnp.float32) * lam_k1.astype(jnp.float32)))
           - jnp.exp(jnp.sum(lam_q2.astype(jnp.float32) * lam_k2.astype(jnp.float32)))
           + lambda_init)
    o_diff = od[:, :, :, 0] - lam.astype(od.dtype) * od[:, :, :, 1]
    o_diff = rmsnorm(o_diff, g_subln) * (1.0 - lambda_init)
    o_diff = o_diff.reshape(B, S, DIFF_OUT)

    x = x + jnp.concatenate([o_mla, o_diff], axis=-1) @ w_out

    hf = rmsnorm(x, g_ffn).reshape(B * S, D)
    x = x + moe(hf, w_router, b_router, w_gate, b_gate, w_up, b_up, w_down, b_down).reshape(B, S, D)
    return x


def setup_inputs(seed: int = 0) -> dict:
    key = jax.random.key(seed)
    ks = jax.random.split(key, 26)
    f32 = jnp.float32

    def nrm(k, shape, scale):
        return jax.random.normal(k, shape, f32) * scale

    def gain(k, shape):
        return 1.0 + 0.05 * jax.random.normal(k, shape, f32)

    L = DEPTH
    return {
        "x_prompt": nrm(ks[0], (BATCH, SEQ, D_MODEL), 1.0),
        "x_sample": nrm(ks[1], (DEC_BATCH, DEC_SEQ, D_MODEL), 1.0),
        "g_attn": gain(ks[2], (L, D_MODEL)),
        "w_in": nrm(ks[3], (L, D_MODEL, P_IN), D_MODEL ** -0.5),
        "g_cq": gain(ks[4], (L, Q_LORA)),
        "w_uq": nrm(ks[5], (L, Q_LORA, MLA_HEADS * (MLA_NOPE + MLA_ROPE)), Q_LORA ** -0.5),
        "g_ckv": gain(ks[6], (L, KV_LORA)),
        "w_uk": nrm(ks[7], (L, KV_LORA, MLA_HEADS * MLA_NOPE), KV_LORA ** -0.5),
        "w_uv": nrm(ks[8], (L, KV_LORA, MLA_HEADS * MLA_V), KV_LORA ** -0.5),
        "lam_q1": nrm(ks[9], (L, DIFF_HD), 0.1),
        "lam_k1": nrm(ks[10], (L, DIFF_HD), 0.1),
        "lam_q2": nrm(ks[11], (L, DIFF_HD), 0.1),
        "lam_k2": nrm(ks[12], (L, DIFF_HD), 0.1),
        "g_subln": gain(ks[13], (L, 2 * DIFF_HD)),
        "w_out": nrm(ks[14], (L, MIX_W, D_MODEL), MIX_W ** -0.5),
        "g_ffn": gain(ks[15], (L, D_MODEL)),
        "w_router": nrm(ks[16], (L, D_MODEL, N_EXPERTS), D_MODEL ** -0.5),
        "b_router": nrm(ks[17], (L, N_EXPERTS), 0.01),
        "w_gate": nrm(ks[18], (L, N_EXPERTS, D_MODEL, D_FF), D_MODEL ** -0.5),
        "b_gate": nrm(ks[19], (L, N_EXPERTS, D_FF), 0.01),
        "w_up": nrm(ks[20], (L, N_EXPERTS, D_MODEL, D_FF), D_MODEL ** -0.5),
        "b_up": nrm(ks[21], (L, N_EXPERTS, D_FF), 0.01),
        "w_down": nrm(ks[22], (L, N_EXPERTS, D_FF, D_MODEL), D_FF ** -0.5),
        "b_down": nrm(ks[23], (L, N_EXPERTS, D_MODEL), 0.01),
        "g_final": gain(ks[24], (D_MODEL,)),
    }


def reference(x_prompt, x_sample, g_attn, w_in, g_cq, w_uq, g_ckv, w_uk, w_uv,
              lam_q1, lam_k1, lam_q2, lam_k2, g_subln, w_out, g_ffn,
              w_router, b_router, w_gate, b_gate, w_up, b_up, w_down, b_down, g_final):
    def trunk(x):
        for l in range(DEPTH):
            x = encoder_layer(x, l, g_attn[l], w_in[l], g_cq[l], w_uq[l], g_ckv[l], w_uk[l], w_uv[l],
                              lam_q1[l], lam_k1[l], lam_q2[l], lam_k2[l], g_subln[l], w_out[l], g_ffn[l],
                              w_router[l], b_router[l], w_gate[l], b_gate[l], w_up[l], b_up[l],
                              w_down[l], b_down[l])
        return rmsnorm(x, g_final)

    y_prompt = trunk(x_prompt)
    y_sample = trunk(x_sample)
    return (y_prompt, y_sample)
```

```python
import functools
import math

import jax
import jax.numpy as jnp
from jax import lax
from jax.experimental import pallas as pl
from jax.experimental.pallas import tpu as pltpu

F32 = jnp.float32
BF16 = jnp.bfloat16
I32 = jnp.int32

D_MODEL = 1024
MLA_HEADS = 8
MLA_NOPE = 64
MLA_ROPE = 32
MLA_V = 64
Q_LORA = 384
KV_LORA = 256
MLA_THETA = 10000.0
DIFF_HEADS = 4
DIFF_HD = 64
DIFF_ROT = DIFF_HD // 4
DIFF_THETA = 500000.0
DIFF_QK = DIFF_HEADS * 2 * DIFF_HD
DIFF_VW = DIFF_HEADS * 2 * DIFF_HD
N_EXPERTS = 32
TOP_K = 4
D_FF = 1024
SWIGLU_ALPHA = 1.702
SWIGLU_LIMIT = 7.0
RMS_EPS = 1e-5

LANES = 128
HEAD_PAD = 128
P_IN_PAD = Q_LORA + KV_LORA + LANES + 2 * DIFF_QK + DIFF_VW
OFF_CKV = Q_LORA
OFF_KPE = Q_LORA + KV_LORA
OFF_DQ = OFF_KPE + LANES
OFF_DK = OFF_DQ + DIFF_QK
OFF_DV = OFF_DK + DIFF_QK
MLA_SCALE = 1.0 / math.sqrt(MLA_NOPE + MLA_ROPE)
DIFF_SCALE = 1.0 / math.sqrt(DIFF_HD)

TOK_BLOCK = 512
Q_TILE = 256
MOE_TILE = 512
DISPATCH_BLOCK = 1024
COMBINE_BLOCK = 256
DMA_LAG = 32
NEG_BIG = -1e30
VMEM_LIMIT = 56 * 1024 * 1024


def _rms(x, g):
    ms = jnp.mean(x * x, axis=-1, keepdims=True)
    return x * lax.rsqrt(ms + RMS_EPS) * g


def _cparams(sem):
    return pltpu.CompilerParams(dimension_semantics=sem, vmem_limit_bytes=VMEM_LIMIT)


def _proj_kernel(x_ref, ga_ref, win_ref, gcq_ref, wuq_ref, gckv_ref, wuk_ref, wuv_ref,
                 cm_ref, sm_ref, cd_ref, sd_ref,
                 q_ref, k_ref, vt_ref, dq_ref, dk_ref, dvt_ref):
    x = x_ref[...]
    tm = x.shape[0]
    h = _rms(x, ga_ref[...]).astype(BF16)
    proj = jnp.dot(h, win_ref[...], preferred_element_type=F32)

    lane = lax.broadcasted_iota(I32, (tm, LANES), 1)
    first_m = (lane & (MLA_ROPE // 2)) == 0
    first_d = (lane & (DIFF_ROT // 2)) == 0
    cm, sm, cd, sd = cm_ref[...], sm_ref[...], cd_ref[...], sd_ref[...]

    def rope_m(v):
        partner = jnp.where(first_m, pltpu.roll(v, LANES - MLA_ROPE // 2, 1),
                            pltpu.roll(v, MLA_ROPE // 2, 1))
        return v * cm + partner * sm

    def rope_d(v):
        partner = jnp.where(first_d, pltpu.roll(v, LANES - DIFF_ROT // 2, 1),
                            pltpu.roll(v, DIFF_ROT // 2, 1))
        return v * cd + partner * sd

    cq = _rms(proj[:, 0:Q_LORA], gcq_ref[...]).astype(BF16)
    q = jnp.dot(cq, wuq_ref[...], preferred_element_type=F32)
    ckv = _rms(proj[:, OFF_CKV:OFF_KPE], gckv_ref[...]).astype(BF16)
    kn = jnp.dot(ckv, wuk_ref[...], preferred_element_type=F32)
    kpe = rope_m(proj[:, OFF_KPE:OFF_DQ])
    for hh in range(MLA_HEADS):
        sl = slice(HEAD_PAD * hh, HEAD_PAD * (hh + 1))
        q_ref[:, sl] = (rope_m(q[:, sl]) * MLA_SCALE).astype(BF16)
        k_ref[:, sl] = (kn[:, sl] + kpe).astype(BF16)
    v = jnp.dot(ckv, wuv_ref[...], preferred_element_type=F32)
    vt_ref[0] = v.T.astype(BF16)

    for i in range(DIFF_QK // LANES):
        sl = slice(LANES * i, LANES * (i + 1))
        dq_ref[:, sl] = (rope_d(proj[:, OFF_DQ + LANES * i:OFF_DQ + LANES * (i + 1)])
                         * DIFF_SCALE).astype(BF16)
        dk_ref[:, sl] = rope_d(proj[:, OFF_DK + LANES * i:OFF_DK + LANES * (i + 1)]).astype(BF16)
    dvt_ref[0] = proj[:, OFF_DV:OFF_DV + DIFF_VW].T.astype(BF16)


def _proj_call(x2d, S, w):
    T = x2d.shape[0]
    tm = TOK_BLOCK
    nblk = T // tm
    spb = S // tm
    full = lambda shp: pl.BlockSpec(shp, lambda i: (0,) * len(shp))
    tab = pl.BlockSpec((tm, LANES), lambda i: (i % spb, 0))
    tokb = lambda n: pl.BlockSpec((tm, n), lambda i: (i, 0))
    vtb = pl.BlockSpec((1, DIFF_VW, tm), lambda i: (i, 0, 0))
    return pl.pallas_call(
        _proj_kernel,
        grid=(nblk,),
        in_specs=[tokb(D_MODEL), full((1, D_MODEL)), full((D_MODEL, P_IN_PAD)),
                  full((1, Q_LORA)), full((Q_LORA, MLA_HEADS * HEAD_PAD)),
                  full((1, KV_LORA)), full((KV_LORA, MLA_HEADS * HEAD_PAD)),
                  full((KV_LORA, MLA_HEADS * MLA_V)), tab, tab, tab, tab],
        out_specs=[tokb(MLA_HEADS * HEAD_PAD), tokb(MLA_HEADS * HEAD_PAD), vtb,
                   tokb(DIFF_QK), tokb(DIFF_QK), vtb],
        out_shape=[jax.ShapeDtypeStruct((T, MLA_HEADS * HEAD_PAD), BF16),
                   jax.ShapeDtypeStruct((T, MLA_HEADS * HEAD_PAD), BF16),
                   jax.ShapeDtypeStruct((nblk, MLA_HEADS * MLA_V, tm), BF16),
                   jax.ShapeDtypeStruct((T, DIFF_QK), BF16),
                   jax.ShapeDtypeStruct((T, DIFF_QK), BF16),
                   jax.ShapeDtypeStruct((nblk, DIFF_VW, tm), BF16)],
        compiler_params=_cparams(("arbitrary",)),
        name="proj",
    )(x2d, w["g_attn"], w["w_in"], w["g_cq"], w["w_uq"], w["g_ckv"], w["w_uk"], w["w_uv"],
      w["cos_m"], w["sin_m"], w["cos_d"], w["sin_d"])


def _attn_kernel(*refs, mla, n_chunks, kc, lambda_init):
    if mla:
        q_ref, k_ref, vt_ref, o_ref = refs
    else:
        lam_ref, g_ref, q_ref, k_ref, vt_ref, o_ref = refs
    tq = q_ref.shape[0]
    dv = MLA_V if mla else 2 * DIFF_HD
    outs = []
    for m in range(2):
        if mla:
            q = q_ref[:, HEAD_PAD * m:HEAD_PAD * (m + 1)]
        else:
            lane = lax.broadcasted_iota(I32, (tq, LANES), 1)
            keep = (lane < DIFF_HD) if m == 0 else (lane >= DIFF_HD)
            qf = q_ref[...]
            q = jnp.where(keep, qf, jnp.zeros_like(qf))

        def body(c, carry, q=q, m=m):
            m_i, l_i, acc = carry
            r0 = pl.multiple_of(c * kc, kc)
            if mla:
                kb = k_ref[pl.ds(r0, kc), HEAD_PAD * m:HEAD_PAD * (m + 1)]
                vb = vt_ref[c, MLA_V * m:MLA_V * (m + 1), :]
            else:
                kb = k_ref[pl.ds(r0, kc), :]
                vb = vt_ref[c]
            s = lax.dot_general(kb, q, (((1,), (1,)), ((), ())),
                                preferred_element_type=F32)
            m_new = jnp.maximum(m_i, jnp.max(s, axis=0, keepdims=True))
            alpha = jnp.exp(m_i - m_new)
            p = jnp.exp(s - m_new)
            l_new = alpha * l_i + jnp.sum(p, axis=0, keepdims=True)
            acc_new = alpha * acc + jnp.dot(vb, p.astype(BF16), preferred_element_type=F32)
            return m_new, l_new, acc_new

        init = (jnp.full((1, tq), -jnp.inf, F32), jnp.zeros((1, tq), F32),
                jnp.zeros((dv, tq), F32))
        _, l_f, acc = lax.fori_loop(0, n_chunks, body, init)
        outs.append(acc / l_f)

    if mla:
        o = jnp.concatenate(outs, axis=0)
    else:
        a = lam_ref[...]
        s1 = jnp.sum(a[0:1, :] * a[1:2, :], axis=1, keepdims=True)
        s2 = jnp.sum(a[2:3, :] * a[3:4, :], axis=1, keepdims=True)
        lam = jnp.exp(s1) - jnp.exp(s2) + lambda_init
        o = outs[0] - lam * outs[1]
        ms = jnp.mean(o * o, axis=0, keepdims=True)
        o = o * lax.rsqrt(ms + RMS_EPS) * g_ref[...] * (1.0 - lambda_init)
    o_ref[...] = o.T.astype(BF16)


def _attn_call(q, k, vt, B, S, *, mla, lam=None, g=None, lambda_init=0.0):
    T = q.shape[0]
    tq = Q_TILE
    nq = S // tq
    kc = vt.shape[2]
    n_chunks = S // kc
    qw = 2 * HEAD_PAD if mla else LANES
    in_specs = [pl.BlockSpec((tq, qw), lambda b, j, i: (b * nq + i, j)),
                pl.BlockSpec((S, qw), lambda b, j, i: (b, j)),
                pl.BlockSpec((n_chunks, LANES, kc), lambda b, j, i: (b, j, 0))]
    args = [q, k, vt]
    if not mla:
        in_specs = [pl.BlockSpec((8, LANES), lambda b, j, i: (0, 0)),
                    pl.BlockSpec((2 * DIFF_HD, 1), lambda b, j, i: (0, 0))] + in_specs
        args = [lam, g] + args
    return pl.pallas_call(
        functools.partial(_attn_kernel, mla=mla, n_chunks=n_chunks, kc=kc,
                          lambda_init=lambda_init),
        grid=(B, 4, nq),
        in_specs=in_specs,
        out_specs=pl.BlockSpec((tq, LANES), lambda b, j, i: (b * nq + i, j)),
        out_shape=jax.ShapeDtypeStruct((T, 4 * LANES), BF16),
        compiler_params=_cparams(("arbitrary", "arbitrary", "arbitrary")),
        name="attn_mla" if mla else "attn_diff",
    )(*args)


def _post_kernel(x_ref, om_ref, od_ref, woa_ref, wob_ref, gf_ref, wrh_ref, wrl_ref, br_ref,
                 x2_ref, hf_ref, ri_ref, rg_ref, cnt_ref, base_ref):
    step = pl.program_id(0)

    @pl.when(step == 0)
    def _():
        base_ref[...] = jnp.zeros_like(base_ref)

    x2 = (x_ref[...]
          + jnp.dot(om_ref[...], woa_ref[...], preferred_element_type=F32)
          + jnp.dot(od_ref[...], wob_ref[...], preferred_element_type=F32))
    x2_ref[...] = x2
    hf = _rms(x2, gf_ref[...])
    hf_ref[...] = hf
    tm = hf.shape[0]

    h_hi = hf.astype(BF16)
    h_lo = (hf - h_hi.astype(F32)).astype(BF16)
    logits = (jnp.dot(h_hi, wrh_ref[...], preferred_element_type=F32)
              + jnp.dot(h_lo, wrh_ref[...], preferred_element_type=F32)
              + jnp.dot(h_hi, wrl_ref[...], preferred_element_type=F32)
              + br_ref[...])

    lane = lax.broadcasted_iota(I32, (tm, LANES), 1).astype(F32)
    work = logits
    sels, vals, idxs = [], [], []
    for _ in range(TOP_K):
        mx = jnp.max(work, axis=1, keepdims=True)
        idx = jnp.min(jnp.where(work == mx, lane, float(LANES)), axis=1, keepdims=True)
        sel = lane == idx
        work = jnp.where(sel, -jnp.inf, work)
        sels.append(sel)
        vals.append(mx)
        idxs.append(idx)
    es = [jnp.exp(v - vals[0]) for v in vals]
    den = es[0] + es[1] + es[2] + es[3]
    gates = [e / den for e in es]

    multi = jnp.zeros((tm, LANES), F32)
    for sel in sels:
        multi = multi + jnp.where(sel, 1.0, 0.0)
    row = lax.broadcasted_iota(I32, (tm, tm), 0)
    col = lax.broadcasted_iota(I32, (tm, tm), 1)
    lower = jnp.where(row > col, 1.0, 0.0).astype(BF16)
    prefix = jnp.dot(lower, multi.astype(BF16), preferred_element_type=F32) + base_ref[0:1, :]
    ri = jnp.zeros((tm, LANES), F32)
    rg = jnp.zeros((tm, LANES), F32)
    for kk in range(TOP_K):
        rank = jnp.sum(jnp.where(sels[kk], prefix, 0.0), axis=1, keepdims=True)
        ri = jnp.where(lane == float(kk), idxs[kk], ri)
        ri = jnp.where(lane == float(TOP_K + kk), rank, ri)
        rg = jnp.where(lane == float(kk), gates[kk], rg)
    ri_ref[...] = ri.astype(I32)
    rg_ref[...] = rg
    new_base = base_ref[0:1, :] + jnp.sum(multi, axis=0, keepdims=True)
    base_ref[0:1, :] = new_base
    cnt_ref[...] = jnp.broadcast_to(new_base, cnt_ref.shape)


def _post_call(x2d, o_mla, o_diff, w):
    T = x2d.shape[0]
    tm = TOK_BLOCK
    full = lambda shp: pl.BlockSpec(shp, lambda i: (0,) * len(shp))
    tokb = lambda n: pl.BlockSpec((tm, n), lambda i: (i, 0))
    return pl.pallas_call(
        _post_kernel,
        grid=(T // tm,),
        in_specs=[tokb(D_MODEL), tokb(4 * LANES), tokb(4 * LANES),
                  full((4 * LANES, D_MODEL)), full((4 * LANES, D_MODEL)), full((1, D_MODEL)),
                  full((D_MODEL, LANES)), full((D_MODEL, LANES)), full((1, LANES))],
        out_specs=[tokb(D_MODEL), tokb(D_MODEL), tokb(LANES), tokb(LANES), full((8, LANES))],
        out_shape=[jax.ShapeDtypeStruct((T, D_MODEL), F32),
                   jax.ShapeDtypeStruct((T, D_MODEL), F32),
                   jax.ShapeDtypeStruct((T, LANES), I32),
                   jax.ShapeDtypeStruct((T, LANES), F32),
                   jax.ShapeDtypeStruct((8, LANES), F32)],
        scratch_shapes=[pltpu.VMEM((8, LANES), F32)],
        compiler_params=_cparams(("arbitrary",)),
        name="post",
    )(x2d, o_mla, o_diff, w["w_out_a"], w["w_out_b"], w["g_ffn"],
      w["w_router_hi"], w["w_router_lo"], w["b_router"])


def _row_copy(src, si, dst, di, sem):
    return pltpu.make_async_copy(src.at[pl.ds(si, 1), :], dst.at[pl.ds(di, 1), :], sem)


def _dispatch_kernel(pend_ref, npad_ref, dest_ref, hf_ref, xs_ref, zero_ref, sem, zsem):
    step = pl.program_id(0)
    tb = dest_ref.shape[1]

    @pl.when(step == 0)
    def _():
        zero_ref[...] = jnp.zeros_like(zero_ref)

        def zcopy(e):
            start = pl.multiple_of(pend_ref[e] - MOE_TILE, MOE_TILE)
            return pltpu.make_async_copy(zero_ref, xs_ref.at[pl.ds(start, MOE_TILE), :], zsem)

        def zstart(e, c):
            @pl.when(npad_ref[e] > 0)
            def _():
                zcopy(e).start()
            return c

        def zwait(e, c):
            @pl.when(npad_ref[e] > 0)
            def _():
                zcopy(e).wait()
            return c

        lax.fori_loop(0, N_EXPERTS, zstart, 0)
        lax.fori_loop(0, N_EXPERTS, zwait, 0)

    def start_tok(t):
        for kk in range(TOP_K):
            _row_copy(hf_ref, step * tb + t, xs_ref, dest_ref[kk, t], sem).start()

    def wait_tok():
        for kk in range(TOP_K):
            _row_copy(hf_ref, 0, xs_ref, 0, sem).wait()

    def body(t, c):
        start_tok(t)

        @pl.when(t >= DMA_LAG)
        def _():
            wait_tok()
        return c

    lax.fori_loop(0, tb, body, 0)

    def drain(t, c):
        wait_tok()
        return c

    lax.fori_loop(0, DMA_LAG, drain, 0)


def _dispatch_call(pad_end, n_pad, dest_t, hf, n_slots):
    T = hf.shape[0]
    tb = DISPATCH_BLOCK
    return pl.pallas_call(
        _dispatch_kernel,
        grid_spec=pltpu.PrefetchScalarGridSpec(
            num_scalar_prefetch=2,
            grid=(T // tb,),
            in_specs=[pl.BlockSpec((TOP_K, tb), lambda i, pe, npd: (0, i),
                                   memory_space=pltpu.SMEM),
                      pl.BlockSpec(memory_space=pl.ANY)],
            out_specs=pl.BlockSpec(memory_space=pl.ANY),
            scratch_shapes=[pltpu.VMEM((MOE_TILE, D_MODEL), F32),
                            pltpu.SemaphoreType.DMA, pltpu.SemaphoreType.DMA]),
        out_shape=jax.ShapeDtypeStruct((n_slots, D_MODEL), F32),
        compiler_params=_cparams(("arbitrary",)),
        name="dispatch",
    )(pad_end, n_pad, dest_t, hf)


def _expert_kernel(be_ref, xs_ref, wg_ref, bg_ref, wu_ref, bu_ref, wd_ref, bd_ref, ys_ref,
                   wgb, wub, wdb):
    i = pl.program_id(0)
    prev = be_ref[jnp.maximum(i - 1, 0)]

    @pl.when(jnp.logical_or(i == 0, be_ref[i] != prev))
    def _():
        wgb[...] = wg_ref[0].astype(BF16)
        wub[...] = wu_ref[0].astype(BF16)
        wdb[...] = wd_ref[0].astype(BF16)

    x = xs_ref[...].astype(BF16)
    g = jnp.dot(x, wgb[...], preferred_element_type=F32) + bg_ref[0]
    u = jnp.dot(x, wub[...], preferred_element_type=F32) + bu_ref[0]
    g = jnp.minimum(g, SWIGLU_LIMIT)
    u = jnp.clip(u, -SWIGLU_LIMIT, SWIGLU_LIMIT)
    a = g * (1.0 / (1.0 + jnp.exp(-SWIGLU_ALPHA * g))) * (u + 1.0)
    ys_ref[...] = jnp.dot(a.astype(BF16), wdb[...], preferred_element_type=F32) + bd_ref[0]


def _expert_call(block_e, xs, w):
    n_slots = xs.shape[0]
    wspec = lambda r, c: pl.BlockSpec((1, r, c), lambda i, be: (be[i], 0, 0))
    slotb = pl.BlockSpec((MOE_TILE, D_MODEL), lambda i, be: (i, 0))
    return pl.pallas_call(
        _expert_kernel,
        grid_spec=pltpu.PrefetchScalarGridSpec(
            num_scalar_prefetch=1,
            grid=(n_slots // MOE_TILE,),
            in_specs=[slotb, wspec(D_MODEL, D_FF), wspec(1, D_FF), wspec(D_MODEL, D_FF),
                      wspec(1, D_FF), wspec(D_FF, D_MODEL), wspec(1, D_MODEL)],
            out_specs=slotb,
            scratch_shapes=[pltpu.VMEM((D_MODEL, D_FF), BF16), pltpu.VMEM((D_MODEL, D_FF), BF16),
                            pltpu.VMEM((D_FF, D_MODEL), BF16)]),
        out_shape=jax.ShapeDtypeStruct((n_slots, D_MODEL), F32),
        compiler_params=_cparams(("arbitrary",)),
        name="experts",
    )(block_e, xs, w["w_gate"], w["b_gate"], w["w_up"], w["b_up"], w["w_down"], w["b_down"])


def _combine_kernel(dest_ref, rg_ref, x2_ref, gfin_ref, ys_ref, out_ref, buf, sem, *, final):
    tb = x2_ref.shape[0]

    def start_tok(t):
        for kk in range(TOP_K):
            _row_copy(ys_ref, dest_ref[kk, t], buf.at[kk], t, sem).start()

    def wait_tok():
        for kk in range(TOP_K):
            _row_copy(ys_ref, 0, buf.at[kk], 0, sem).wait()

    def body(t, c):
        start_tok(t)

        @pl.when(t >= DMA_LAG)
        def _():
            wait_tok()
        return c

    lax.fori_loop(0, tb, body, 0)

    def drain(t, c):
        wait_tok()
        return c

    lax.fori_loop(0, DMA_LAG, drain, 0)

    rg = rg_ref[...]
    y = x2_ref[...]
    for kk in range(TOP_K):
        y = y + buf[kk] * rg[:, kk:kk + 1]
    if final:
        y = _rms(y, gfin_ref[...])
    out_ref[...] = y


def _combine_call(dest_t, rg, x2, g_final, ys, *, final):
    T = x2.shape[0]
    tb = COMBINE_BLOCK
    return pl.pallas_call(
        functools.partial(_combine_kernel, final=final),
        grid=(T // tb,),
        in_specs=[pl.BlockSpec((TOP_K, tb), lambda i: (0, i), memory_space=pltpu.SMEM),
                  pl.BlockSpec((tb, LANES), lambda i: (i, 0)),
                  pl.BlockSpec((tb, D_MODEL), lambda i: (i, 0)),
                  pl.BlockSpec((1, D_MODEL), lambda i: (0, 0)),
                  pl.BlockSpec(memory_space=pl.ANY)],
        out_specs=pl.BlockSpec((tb, D_MODEL), lambda i: (i, 0)),
        out_shape=jax.ShapeDtypeStruct((T, D_MODEL), F32),
        scratch_shapes=[pltpu.VMEM((TOP_K, tb, D_MODEL), F32), pltpu.SemaphoreType.DMA],
        compiler_params=_cparams(("arbitrary",)),
        name="combine",
    )(dest_t, rg, x2, g_final, ys)


def _rope_tables(S, rot_dim, theta, group, lead):
    half = rot_dim // 2
    inv_freq = jnp.power(theta, -2.0 * jnp.arange(half, dtype=F32) / rot_dim)
    ang = jnp.arange(S, dtype=F32)[:, None] * inv_freq[None, :]
    cos = jnp.cos(ang)
    sin = jnp.sin(ang)
    rest = group - lead - rot_dim
    cg = jnp.concatenate([jnp.ones((S, lead), F32), cos, cos, jnp.ones((S, rest), F32)], axis=1)
    sg = jnp.concatenate([jnp.zeros((S, lead), F32), -sin, sin, jnp.zeros((S, rest), F32)], axis=1)
    reps = LANES // group
    return jnp.tile(cg, (1, reps)), jnp.tile(sg, (1, reps))


def _prep_layer(l, S, g_attn, w_in, g_cq, w_uq, g_ckv, w_uk, w_uv, lam_q1, lam_k1, lam_q2,
                lam_k2, g_subln, w_out, g_ffn, w_router, b_router, w_gate, b_gate, w_up, b_up,
                w_down, b_down):
    w = {}
    wi = w_in[l]
    kpe_cols = jnp.zeros((D_MODEL, LANES), F32).at[:, MLA_NOPE:MLA_NOPE + MLA_ROPE].set(
        wi[:, OFF_KPE:OFF_KPE + MLA_ROPE])
    w["w_in"] = jnp.concatenate(
        [wi[:, :OFF_KPE], kpe_cols, wi[:, OFF_KPE + MLA_ROPE:]], axis=1).astype(BF16)
    dqk = MLA_NOPE + MLA_ROPE
    wq = w_uq[l].reshape(Q_LORA, MLA_HEADS, dqk)
    w["w_uq"] = jnp.pad(wq, ((0, 0), (0, 0), (0, HEAD_PAD - dqk))).reshape(
        Q_LORA, MLA_HEADS * HEAD_PAD).astype(BF16)
    wk = w_uk[l].reshape(KV_LORA, MLA_HEADS, MLA_NOPE)
    w["w_uk"] = jnp.pad(wk, ((0, 0), (0, 0), (0, HEAD_PAD - MLA_NOPE))).reshape(
        KV_LORA, MLA_HEADS * HEAD_PAD).astype(BF16)
    w["w_uv"] = w_uv[l].astype(BF16)
    w["g_attn"] = g_attn[l].reshape(1, D_MODEL)
    w["g_cq"] = g_cq[l].reshape(1, Q_LORA)
    w["g_ckv"] = g_ckv[l].reshape(1, KV_LORA)
    w["cos_m"], w["sin_m"] = _rope_tables(S, MLA_ROPE, MLA_THETA, LANES, MLA_NOPE)
    w["cos_d"], w["sin_d"] = _rope_tables(S, DIFF_ROT, DIFF_THETA, DIFF_HD, 0)
    lam = jnp.stack([lam_q1[l], lam_k1[l], lam_q2[l], lam_k2[l]]).astype(F32)
    w["lam"] = jnp.pad(lam, ((0, 8 - 4), (0, LANES - DIFF_HD)))
    w["g_subln"] = g_subln[l].reshape(2 * DIFF_HD, 1)
    wo = w_out[l].astype(BF16)
    w["w_out_a"] = wo[:MLA_HEADS * MLA_V]
    w["w_out_b"] = wo[MLA_HEADS * MLA_V:]
    w["g_ffn"] = g_ffn[l].reshape(1, D_MODEL)
    wr = jnp.pad(w_router[l], ((0, 0), (0, LANES - N_EXPERTS)))
    w["w_router_hi"] = wr.astype(BF16)
    w["w_router_lo"] = (wr - w["w_router_hi"].astype(F32)).astype(BF16)
    w["b_router"] = jnp.pad(b_router[l].astype(F32), (0, LANES - N_EXPERTS),
                            constant_values=NEG_BIG).reshape(1, LANES)
    w["w_gate"], w["w_up"], w["w_down"] = w_gate[l], w_up[l], w_down[l]
    w["b_gate"] = b_gate[l].reshape(N_EXPERTS, 1, D_FF)
    w["b_up"] = b_up[l].reshape(N_EXPERTS, 1, D_FF)
    w["b_down"] = b_down[l].reshape(N_EXPERTS, 1, D_MODEL)
    w["lambda_init"] = 0.8 - 0.6 * math.exp(-0.3 * l)
    return w


def _layer(x2d, B, S, w, g_final, *, final):
    T = x2d.shape[0]
    q, k, vt, dq, dk, dvt = _proj_call(x2d, S, w)
    o_mla = _attn_call(q, k, vt, B, S, mla=True)
    o_diff = _attn_call(dq, dk, dvt, B, S, mla=False, lam=w["lam"], g=w["g_subln"],
                        lambda_init=w["lambda_init"])
    x2, hf, ri, rg, cnt = _post_call(x2d, o_mla, o_diff, w)

    counts = cnt[0, :N_EXPERTS].astype(I32)
    padded = ((counts + MOE_TILE - 1) // MOE_TILE) * MOE_TILE
    pad_end = jnp.cumsum(padded)
    pad_start = pad_end - padded
    idx = ri[:, :TOP_K]
    dest_t = (pad_start[idx] + ri[:, TOP_K:2 * TOP_K]).T
    n_blocks = (T * TOP_K) // MOE_TILE + N_EXPERTS
    n_slots = n_blocks * MOE_TILE
    block_e = jnp.minimum(
        jnp.searchsorted(pad_end, jnp.arange(n_blocks, dtype=I32) * MOE_TILE, side="right"),
        N_EXPERTS - 1).astype(I32)

    xs = _dispatch_call(pad_end.astype(I32), (padded - counts).astype(I32), dest_t, hf, n_slots)
    ys = _expert_call(block_e, xs, w)
    return _combine_call(dest_t, rg, x2, g_final, ys, final=final)


def kernel(x_prompt, x_sample, g_attn, w_in, g_cq, w_uq, g_ckv, w_uk, w_uv, lam_q1, lam_k1,
           lam_q2, lam_k2, g_subln, w_out, g_ffn, w_router, b_router, w_gate, b_gate, w_up, b_up,
           w_down, b_down, g_final):
    depth = w_in.shape[0]
    gfin = g_final.reshape(1, D_MODEL)

    def trunk(x):
        B, S, D = x.shape
        assert D == D_MODEL and S % TOK_BLOCK == 0 and (B * S) % DISPATCH_BLOCK == 0
        x2d = x.reshape(B * S, D)
        for l in range(depth):
            w = _prep_layer(l, S, g_attn, w_in, g_cq, w_uq, g_ckv, w_uk, w_uv, lam_q1, lam_k1,
                            lam_q2, lam_k2, g_subln, w_out, g_ffn, w_router, b_router, w_gate,
                            b_gate, w_up, b_up, w_down, b_down)
            x2d = _layer(x2d, B, S, w, gfin, final=(l == depth - 1))
        return x2d.reshape(B, S, D)

    return (trunk(x_prompt), trunk(x_sample))
```

```python
import functools
import math

import jax
import jax.numpy as jnp
from jax import lax
from jax.experimental import pallas as pl
from jax.experimental.pallas import tpu as pltpu

F32 = jnp.float32
BF16 = jnp.bfloat16
I32 = jnp.int32

D_MODEL = 1024
MLA_HEADS = 8
MLA_NOPE = 64
MLA_ROPE = 32
MLA_V = 64
Q_LORA = 384
KV_LORA = 256
MLA_THETA = 10000.0
DIFF_HEADS = 4
DIFF_HD = 64
DIFF_ROT = DIFF_HD // 4
DIFF_THETA = 500000.0
DIFF_QK = DIFF_HEADS * 2 * DIFF_HD
DIFF_VW = DIFF_HEADS * 2 * DIFF_HD
N_EXPERTS = 32
TOP_K = 4
D_FF = 1024
SWIGLU_ALPHA = 1.702
SWIGLU_LIMIT = 7.0
RMS_EPS = 1e-5

LANES = 128
HEAD_PAD = 128
P_IN_PAD = Q_LORA + KV_LORA + LANES + 2 * DIFF_QK + DIFF_VW
OFF_CKV = Q_LORA
OFF_KPE = Q_LORA + KV_LORA
OFF_DQ = OFF_KPE + LANES
OFF_DK = OFF_DQ + DIFF_QK
OFF_DV = OFF_DK + DIFF_QK
LOG2_E = math.log2(math.e)
MLA_SCALE = LOG2_E / math.sqrt(MLA_NOPE + MLA_ROPE)
DIFF_SCALE = LOG2_E / math.sqrt(DIFF_HD)

TOK_BLOCK = 512
Q_TILE = 512
Q_SUB = 256
MOE_TILE = 512
COMBINE_BLOCK = 256
ISSUE_UNROLL = 8
WAIT_ROWS = 128
SCORE_LOOKAHEAD = 6
ONES_ROWS = 16
NEG_BIG = -1e30
VMEM_LIMIT = 56 * 1024 * 1024


def _rms(x, g):
    ms = jnp.mean(x * x, axis=-1, keepdims=True)
    return x * lax.rsqrt(ms + RMS_EPS) * g


def _cparams(sem):
    return pltpu.CompilerParams(dimension_semantics=sem, vmem_limit_bytes=VMEM_LIMIT)


def _proj_kernel(x_ref, ga_ref, win_ref, gcq_ref, wuq_ref, gckv_ref, wuk_ref, wuv_ref,
                 cm_ref, sm_ref, cd_ref, sd_ref,
                 q_ref, k_ref, vt_ref, dq_ref, dk_ref, dvt_ref):
    x = x_ref[...]
    tm = x.shape[0]
    h = _rms(x, ga_ref[...]).astype(BF16)
    proj = jnp.dot(h, win_ref[...], preferred_element_type=F32)

    lane = lax.broadcasted_iota(I32, (tm, LANES), 1)
    first_m = (lane & (MLA_ROPE // 2)) == 0
    first_d = (lane & (DIFF_ROT // 2)) == 0
    cm, sm, cd, sd = cm_ref[...], sm_ref[...], cd_ref[...], sd_ref[...]

    def rope_m(v):
        partner = jnp.where(first_m, pltpu.roll(v, LANES - MLA_ROPE // 2, 1),
                            pltpu.roll(v, MLA_ROPE // 2, 1))
        return v * cm + partner * sm

    def rope_d(v):
        partner = jnp.where(first_d, pltpu.roll(v, LANES - DIFF_ROT // 2, 1),
                            pltpu.roll(v, DIFF_ROT // 2, 1))
        return v * cd + partner * sd

    cq = _rms(proj[:, 0:Q_LORA], gcq_ref[...]).astype(BF16)
    q = jnp.dot(cq, wuq_ref[...], preferred_element_type=F32)
    ckv = _rms(proj[:, OFF_CKV:OFF_KPE], gckv_ref[...]).astype(BF16)
    kn = jnp.dot(ckv, wuk_ref[...], preferred_element_type=F32)
    kpe = rope_m(proj[:, OFF_KPE:OFF_DQ])
    for hh in range(MLA_HEADS):
        sl = slice(HEAD_PAD * hh, HEAD_PAD * (hh + 1))
        q_ref[:, sl] = (rope_m(q[:, sl]) * MLA_SCALE).astype(BF16)
        k_ref[:, sl] = (kn[:, sl] + kpe).astype(BF16)
    v = jnp.dot(ckv, wuv_ref[...], preferred_element_type=F32)
    vt_ref[0] = v.T.astype(BF16)

    for i in range(DIFF_QK // LANES):
        sl = slice(LANES * i, LANES * (i + 1))
        dq_ref[:, sl] = (rope_d(proj[:, OFF_DQ + LANES * i:OFF_DQ + LANES * (i + 1)])
                         * DIFF_SCALE).astype(BF16)
        dk_ref[:, sl] = rope_d(proj[:, OFF_DK + LANES * i:OFF_DK + LANES * (i + 1)]).astype(BF16)
    dvt_ref[0] = proj[:, OFF_DV:OFF_DV + DIFF_VW].T.astype(BF16)


def _proj_call(x2d, S, w):
    T = x2d.shape[0]
    tm = TOK_BLOCK
    nblk = T // tm
    spb = S // tm
    full = lambda shp: pl.BlockSpec(shp, lambda i: (0,) * len(shp))
    tab = pl.BlockSpec((tm, LANES), lambda i: (i % spb, 0))
    tokb = lambda n: pl.BlockSpec((tm, n), lambda i: (i, 0))
    vtb = pl.BlockSpec((1, DIFF_VW, tm), lambda i: (i, 0, 0))
    return pl.pallas_call(
        _proj_kernel,
        grid=(nblk,),
        in_specs=[tokb(D_MODEL), full((1, D_MODEL)), full((D_MODEL, P_IN_PAD)),
                  full((1, Q_LORA)), full((Q_LORA, MLA_HEADS * HEAD_PAD)),
                  full((1, KV_LORA)), full((KV_LORA, MLA_HEADS * HEAD_PAD)),
                  full((KV_LORA, MLA_HEADS * MLA_V)), tab, tab, tab, tab],
        out_specs=[tokb(MLA_HEADS * HEAD_PAD), tokb(MLA_HEADS * HEAD_PAD), vtb,
                   tokb(DIFF_QK), tokb(DIFF_QK), vtb],
        out_shape=[jax.ShapeDtypeStruct((T, MLA_HEADS * HEAD_PAD), BF16),
                   jax.ShapeDtypeStruct((T, MLA_HEADS * HEAD_PAD), BF16),
                   jax.ShapeDtypeStruct((nblk, MLA_HEADS * MLA_V, tm), BF16),
                   jax.ShapeDtypeStruct((T, DIFF_QK), BF16),
                   jax.ShapeDtypeStruct((T, DIFF_QK), BF16),
                   jax.ShapeDtypeStruct((nblk, DIFF_VW, tm), BF16)],
        compiler_params=_cparams(("arbitrary",)),
        name="proj",
    )(x2d, w["g_attn"], w["w_in"], w["g_cq"], w["w_uq"], w["g_ckv"], w["w_uk"], w["w_uv"],
      w["cos_m"], w["sin_m"], w["cos_d"], w["sin_d"])


def _attn_kernel(*refs, mla, n_chunks, kc, lambda_init):
    if mla:
        q_ref, k_ref, vt_ref, o_ref = refs
    else:
        lam_ref, g_ref, q_ref, k_ref, vt_ref, o_ref = refs
    dv = MLA_V if mla else 2 * DIFF_HD
    ones = jnp.ones((ONES_ROWS, kc), BF16)
    if not mla:
        a = lam_ref[...]
        s1 = jnp.sum(a[0:1, :] * a[1:2, :], axis=1, keepdims=True)
        s2 = jnp.sum(a[2:3, :] * a[3:4, :], axis=1, keepdims=True)
        lam = jnp.exp(s1) - jnp.exp(s2) + lambda_init

    def query(t, m):
        rows = slice(Q_SUB * t, Q_SUB * (t + 1))
        if mla:
            return q_ref[rows, HEAD_PAD * m:HEAD_PAD * (m + 1)]
        lane = lax.broadcasted_iota(I32, (Q_SUB, LANES), 1)
        keep = (lane < DIFF_HD) if m == 0 else (lane >= DIFF_HD)
        qf = q_ref[rows, :]
        return jnp.where(keep, qf, jnp.zeros_like(qf))

    def scores(t, c, m):
        if mla:
            kb = k_ref[kc * c:kc * (c + 1), HEAD_PAD * m:HEAD_PAD * (m + 1)]
        else:
            kb = k_ref[kc * c:kc * (c + 1), :]
        return lax.dot_general(kb, qs[t][m], (((1,), (1,)), ((), ())),
                               preferred_element_type=F32)

    def finish(t, acc):
        outs = [acc[m][0:dv] / acc[m][dv:dv + 1] for m in range(2)]
        if mla:
            o = jnp.concatenate(outs, axis=0)
        else:
            o = outs[0] - lam * outs[1]
            ms = jnp.mean(o * o, axis=0, keepdims=True)
            o = o * lax.rsqrt(ms + RMS_EPS) * g_ref[...] * (1.0 - lambda_init)
        o_ref[Q_SUB * t:Q_SUB * (t + 1), :] = o.T.astype(BF16)

    n_sub = q_ref.shape[0] // Q_SUB
    qs = [[query(t, m) for m in range(2)] for t in range(n_sub)]
    order = [(t, c, m) for t in range(n_sub) for c in range(n_chunks) for m in range(2)]
    pending = [scores(*order[i]) for i in range(min(SCORE_LOOKAHEAD, len(order)))]
    m_i = acc = None
    for i, (t, c, m) in enumerate(order):
        if c == 0 and m == 0:
            m_i = [jnp.full((1, Q_SUB), -jnp.inf, F32) for _ in range(2)]
            acc = [jnp.zeros((dv + ONES_ROWS, Q_SUB), F32) for _ in range(2)]
        s = pending.pop(0)
        if i + SCORE_LOOKAHEAD < len(order):
            pending.append(scores(*order[i + SCORE_LOOKAHEAD]))
        vb = vt_ref[c, MLA_V * m:MLA_V * (m + 1), :] if mla else vt_ref[c]
        vb = jnp.concatenate([vb, ones], axis=0)
        m_new = jnp.maximum(m_i[m], jnp.max(s, axis=0, keepdims=True))
        alpha = jnp.exp2(m_i[m] - m_new)
        p = jnp.exp2(s - m_new).astype(BF16)
        acc[m] = alpha * acc[m] + jnp.dot(vb, p, preferred_element_type=F32)
        m_i[m] = m_new
        if c == n_chunks - 1 and m == 1:
            finish(t, acc)


def _attn_call(q, k, vt, B, S, *, mla, lam=None, g=None, lambda_init=0.0):
    T = q.shape[0]
    tq = Q_TILE
    nq = S // tq
    kc = vt.shape[2]
    n_chunks = S // kc
    qw = 2 * HEAD_PAD if mla else LANES
    in_specs = [pl.BlockSpec((tq, qw), lambda b, j, i: (b * nq + i, j)),
                pl.BlockSpec((S, qw), lambda b, j, i: (b, j)),
                pl.BlockSpec((n_chunks, LANES, kc), lambda b, j, i: (b, j, 0))]
    args = [q, k, vt]
    if not mla:
        in_specs = [pl.BlockSpec((8, LANES), lambda b, j, i: (0, 0)),
                    pl.BlockSpec((2 * DIFF_HD, 1), lambda b, j, i: (0, 0))] + in_specs
        args = [lam, g] + args
    return pl.pallas_call(
        functools.partial(_attn_kernel, mla=mla, n_chunks=n_chunks, kc=kc,
                          lambda_init=lambda_init),
        grid=(B, 4, nq),
        in_specs=in_specs,
        out_specs=pl.BlockSpec((tq, LANES), lambda b, j, i: (b * nq + i, j)),
        out_shape=jax.ShapeDtypeStruct((T, 4 * LANES), BF16),
        compiler_params=_cparams(("arbitrary", "arbitrary", "arbitrary")),
        name="attn_mla" if mla else "attn_diff",
    )(*args)


def _post_kernel(x_ref, om_ref, od_ref, woa_ref, wob_ref, gf_ref, wrh_ref, wrl_ref, br_ref,
                 x2_ref, hf_ref, ri_ref, rg_ref, cnt_ref, base_ref):
    step = pl.program_id(0)

    @pl.when(step == 0)
    def _():
        base_ref[...] = jnp.zeros_like(base_ref)

    x2 = (x_ref[...]
          + jnp.dot(om_ref[...], woa_ref[...], preferred_element_type=F32)
          + jnp.dot(od_ref[...], wob_ref[...], preferred_element_type=F32))
    x2_ref[...] = x2
    hf = _rms(x2, gf_ref[...])
    hf_ref[...] = hf
    tm = hf.shape[0]

    h_hi = hf.astype(BF16)
    h_lo = (hf - h_hi.astype(F32)).astype(BF16)
    logits = (jnp.dot(h_hi, wrh_ref[...], preferred_element_type=F32)
              + jnp.dot(h_lo, wrh_ref[...], preferred_element_type=F32)
              + jnp.dot(h_hi, wrl_ref[...], preferred_element_type=F32)
              + br_ref[...])

    lane = lax.broadcasted_iota(I32, (tm, LANES), 1).astype(F32)
    work = logits
    sels, vals, idxs = [], [], []
    for _ in range(TOP_K):
        mx = jnp.max(work, axis=1, keepdims=True)
        idx = jnp.min(jnp.where(work == mx, lane, float(LANES)), axis=1, keepdims=True)
        sel = lane == idx
        work = jnp.where(sel, -jnp.inf, work)
        sels.append(sel)
        vals.append(mx)
        idxs.append(idx)
    es = [jnp.exp(v - vals[0]) for v in vals]
    den = es[0] + es[1] + es[2] + es[3]
    gates = [e / den for e in es]

    multi = jnp.zeros((tm, LANES), F32)
    for sel in sels:
        multi = multi + jnp.where(sel, 1.0, 0.0)
    row = lax.broadcasted_iota(I32, (tm, tm), 0)
    col = lax.broadcasted_iota(I32, (tm, tm), 1)
    lower = jnp.where(row > col, 1.0, 0.0).astype(BF16)
    prefix = jnp.dot(lower, multi.astype(BF16), preferred_element_type=F32) + base_ref[0:1, :]
    ri = jnp.zeros((tm, LANES), F32)
    rg = jnp.zeros((tm, LANES), F32)
    for kk in range(TOP_K):
        rank = jnp.sum(jnp.where(sels[kk], prefix, 0.0), axis=1, keepdims=True)
        ri = jnp.where(lane == float(kk), idxs[kk], ri)
        ri = jnp.where(lane == float(TOP_K + kk), rank, ri)
        rg = jnp.where(lane == float(kk), gates[kk], rg)
    ri_ref[...] = ri.astype(I32)
    rg_ref[...] = rg
    new_base = base_ref[0:1, :] + jnp.sum(multi, axis=0, keepdims=True)
    base_ref[0:1, :] = new_base
    cnt_ref[...] = jnp.broadcast_to(new_base, cnt_ref.shape)


def _post_call(x2d, o_mla, o_diff, w):
    T = x2d.shape[0]
    tm = TOK_BLOCK
    full = lambda shp: pl.BlockSpec(shp, lambda i: (0,) * len(shp))
    tokb = lambda n: pl.BlockSpec((tm, n), lambda i: (i, 0))
    return pl.pallas_call(
        _post_kernel,
        grid=(T // tm,),
        in_specs=[tokb(D_MODEL), tokb(4 * LANES), tokb(4 * LANES),
                  full((4 * LANES, D_MODEL)), full((4 * LANES, D_MODEL)), full((1, D_MODEL)),
                  full((D_MODEL, LANES)), full((D_MODEL, LANES)), full((1, LANES))],
        out_specs=[tokb(D_MODEL), tokb(D_MODEL), tokb(LANES), tokb(LANES), full((8, LANES))],
        out_shape=[jax.ShapeDtypeStruct((T, D_MODEL), F32),
                   jax.ShapeDtypeStruct((T, D_MODEL), F32),
                   jax.ShapeDtypeStruct((T, LANES), I32),
                   jax.ShapeDtypeStruct((T, LANES), F32),
                   jax.ShapeDtypeStruct((8, LANES), F32)],
        scratch_shapes=[pltpu.VMEM((8, LANES), F32)],
        compiler_params=_cparams(("arbitrary",)),
        name="post",
    )(x2d, o_mla, o_diff, w["w_out_a"], w["w_out_b"], w["g_ffn"],
      w["w_router_hi"], w["w_router_lo"], w["b_router"])


def _row_copy(src, si, dst, di, sem):
    return pltpu.make_async_copy(src.at[pl.ds(si, 1), :], dst.at[pl.ds(di, 1), :], sem)


def _wait_rows(src, dst, n_rows, sem):
    for r in range(0, n_rows, WAIT_ROWS):
        pltpu.make_async_copy(src.at[pl.ds(0, WAIT_ROWS), :],
                              dst.at[pl.ds(r, WAIT_ROWS), :], sem).wait()


def _dispatch_kernel(tok_ref, hf_ref, xs_ref, sem):
    tb = xs_ref.shape[0]

    def issue(j, c):
        base = pl.multiple_of(j * ISSUE_UNROLL, ISSUE_UNROLL)
        for u in range(ISSUE_UNROLL):
            _row_copy(hf_ref, tok_ref[0, 0, base + u], xs_ref, base + u, sem).start()
        return c

    lax.fori_loop(0, tb // ISSUE_UNROLL, issue, 0)
    _wait_rows(hf_ref, xs_ref, tb, sem)


def _dispatch_call(slot_tok, hf, n_slots):
    tb = MOE_TILE
    return pl.pallas_call(
        _dispatch_kernel,
        grid=(n_slots // tb,),
        in_specs=[pl.BlockSpec((1, 1, tb), lambda i: (i, 0, 0), memory_space=pltpu.SMEM),
                  pl.BlockSpec(memory_space=pl.ANY)],
        out_specs=pl.BlockSpec((tb, D_MODEL), lambda i: (i, 0)),
        out_shape=jax.ShapeDtypeStruct((n_slots, D_MODEL), F32),
        scratch_shapes=[pltpu.SemaphoreType.DMA],
        compiler_params=_cparams(("arbitrary",)),
        name="dispatch",
    )(slot_tok.reshape(n_slots // tb, 1, tb), hf)


def _expert_kernel(be_ref, xs_ref, wg_ref, bg_ref, wu_ref, bu_ref, wd_ref, bd_ref, ys_ref,
                   wgb, wub, wdb):
    i = pl.program_id(0)
    prev = be_ref[jnp.maximum(i - 1, 0)]

    @pl.when(jnp.logical_or(i == 0, be_ref[i] != prev))
    def _():
        wgb[...] = wg_ref[0].astype(BF16)
        wub[...] = wu_ref[0].astype(BF16)
        wdb[...] = wd_ref[0].astype(BF16)

    x = xs_ref[...].astype(BF16)
    g = jnp.dot(x, wgb[...], preferred_element_type=F32) + bg_ref[0]
    u = jnp.dot(x, wub[...], preferred_element_type=F32) + bu_ref[0]
    g = jnp.minimum(g, SWIGLU_LIMIT)
    u = jnp.clip(u, -SWIGLU_LIMIT, SWIGLU_LIMIT)
    a = g * (1.0 / (1.0 + jnp.exp(-SWIGLU_ALPHA * g))) * (u + 1.0)
    ys_ref[...] = jnp.dot(a.astype(BF16), wdb[...], preferred_element_type=F32) + bd_ref[0]


def _expert_call(block_e, xs, w):
    n_slots = xs.shape[0]
    wspec = lambda r, c: pl.BlockSpec((1, r, c), lambda i, be: (be[i], 0, 0))
    slotb = pl.BlockSpec((MOE_TILE, D_MODEL), lambda i, be: (i, 0))
    return pl.pallas_call(
        _expert_kernel,
        grid_spec=pltpu.PrefetchScalarGridSpec(
            num_scalar_prefetch=1,
            grid=(n_slots // MOE_TILE,),
            in_specs=[slotb, wspec(D_MODEL, D_FF), wspec(1, D_FF), wspec(D_MODEL, D_FF),
                      wspec(1, D_FF), wspec(D_FF, D_MODEL), wspec(1, D_MODEL)],
            out_specs=slotb,
            scratch_shapes=[pltpu.VMEM((D_MODEL, D_FF), BF16), pltpu.VMEM((D_MODEL, D_FF), BF16),
                            pltpu.VMEM((D_FF, D_MODEL), BF16)]),
        out_shape=jax.ShapeDtypeStruct((n_slots, D_MODEL), F32),
        compiler_params=_cparams(("arbitrary",)),
        name="experts",
    )(block_e, xs, w["w_gate"], w["b_gate"], w["w_up"], w["b_up"], w["w_down"], w["b_down"])


def _combine_kernel(dest_ref, rg_ref, x2_ref, gfin_ref, ys_ref, out_ref, buf, sem, *, final):
    tb = x2_ref.shape[0]

    def issue(j, c):
        base = pl.multiple_of(j * ISSUE_UNROLL, ISSUE_UNROLL)
        for u in range(ISSUE_UNROLL):
            for kk in range(TOP_K):
                _row_copy(ys_ref, dest_ref[kk, base + u], buf.at[kk], base + u, sem).start()
        return c

    lax.fori_loop(0, tb // ISSUE_UNROLL, issue, 0)
    for kk in range(TOP_K):
        _wait_rows(ys_ref, buf.at[kk], tb, sem)

    rg = rg_ref[...]
    y = x2_ref[...]
    for kk in range(TOP_K):
        y = y + buf[kk] * rg[:, kk:kk + 1]
    if final:
        y = _rms(y, gfin_ref[...])
    out_ref[...] = y


def _combine_call(dest_t, rg, x2, g_final, ys, *, final):
    T = x2.shape[0]
    tb = COMBINE_BLOCK
    return pl.pallas_call(
        functools.partial(_combine_kernel, final=final),
        grid=(T // tb,),
        in_specs=[pl.BlockSpec((TOP_K, tb), lambda i: (0, i), memory_space=pltpu.SMEM),
                  pl.BlockSpec((tb, LANES), lambda i: (i, 0)),
                  pl.BlockSpec((tb, D_MODEL), lambda i: (i, 0)),
                  pl.BlockSpec((1, D_MODEL), lambda i: (0, 0)),
                  pl.BlockSpec(memory_space=pl.ANY)],
        out_specs=pl.BlockSpec((tb, D_MODEL), lambda i: (i, 0)),
        out_shape=jax.ShapeDtypeStruct((T, D_MODEL), F32),
        scratch_shapes=[pltpu.VMEM((TOP_K, tb, D_MODEL), F32), pltpu.SemaphoreType.DMA],
        compiler_params=_cparams(("arbitrary",)),
        name="combine",
    )(dest_t, rg, x2, g_final, ys)


def _rope_tables(S, rot_dim, theta, group, lead):
    half = rot_dim // 2
    inv_freq = jnp.power(theta, -2.0 * jnp.arange(half, dtype=F32) / rot_dim)
    ang = jnp.arange(S, dtype=F32)[:, None] * inv_freq[None, :]
    cos = jnp.cos(ang)
    sin = jnp.sin(ang)
    rest = group - lead - rot_dim
    cg = jnp.concatenate([jnp.ones((S, lead), F32), cos, cos, jnp.ones((S, rest), F32)], axis=1)
    sg = jnp.concatenate([jnp.zeros((S, lead), F32), -sin, sin, jnp.zeros((S, rest), F32)], axis=1)
    reps = LANES // group
    return jnp.tile(cg, (1, reps)), jnp.tile(sg, (1, reps))


def _prep_layer(l, S, g_attn, w_in, g_cq, w_uq, g_ckv, w_uk, w_uv, lam_q1, lam_k1, lam_q2,
                lam_k2, g_subln, w_out, g_ffn, w_router, b_router, w_gate, b_gate, w_up, b_up,
                w_down, b_down):
    w = {}
    wi = w_in[l]
    kpe_cols = jnp.zeros((D_MODEL, LANES), F32).at[:, MLA_NOPE:MLA_NOPE + MLA_ROPE].set(
        wi[:, OFF_KPE:OFF_KPE + MLA_ROPE])
    w["w_in"] = jnp.concatenate(
        [wi[:, :OFF_KPE], kpe_cols, wi[:, OFF_KPE + MLA_ROPE:]], axis=1).astype(BF16)
    dqk = MLA_NOPE + MLA_ROPE
    wq = w_uq[l].reshape(Q_LORA, MLA_HEADS, dqk)
    w["w_uq"] = jnp.pad(wq, ((0, 0), (0, 0), (0, HEAD_PAD - dqk))).reshape(
        Q_LORA, MLA_HEADS * HEAD_PAD).astype(BF16)
    wk = w_uk[l].reshape(KV_LORA, MLA_HEADS, MLA_NOPE)
    w["w_uk"] = jnp.pad(wk, ((0, 0), (0, 0), (0, HEAD_PAD - MLA_NOPE))).reshape(
        KV_LORA, MLA_HEADS * HEAD_PAD).astype(BF16)
    w["w_uv"] = w_uv[l].astype(BF16)
    w["g_attn"] = g_attn[l].reshape(1, D_MODEL)
    w["g_cq"] = g_cq[l].reshape(1, Q_LORA)
    w["g_ckv"] = g_ckv[l].reshape(1, KV_LORA)
    w["cos_m"], w["sin_m"] = _rope_tables(S, MLA_ROPE, MLA_THETA, LANES, MLA_NOPE)
    w["cos_d"], w["sin_d"] = _rope_tables(S, DIFF_ROT, DIFF_THETA, DIFF_HD, 0)
    lam = jnp.stack([lam_q1[l], lam_k1[l], lam_q2[l], lam_k2[l]]).astype(F32)
    w["lam"] = jnp.pad(lam, ((0, 8 - 4), (0, LANES - DIFF_HD)))
    w["g_subln"] = g_subln[l].reshape(2 * DIFF_HD, 1)
    wo = w_out[l].astype(BF16)
    w["w_out_a"] = wo[:MLA_HEADS * MLA_V]
    w["w_out_b"] = wo[MLA_HEADS * MLA_V:]
    w["g_ffn"] = g_ffn[l].reshape(1, D_MODEL)
    wr = jnp.pad(w_router[l], ((0, 0), (0, LANES - N_EXPERTS)))
    w["w_router_hi"] = wr.astype(BF16)
    w["w_router_lo"] = (wr - w["w_router_hi"].astype(F32)).astype(BF16)
    w["b_router"] = jnp.pad(b_router[l].astype(F32), (0, LANES - N_EXPERTS),
                            constant_values=NEG_BIG).reshape(1, LANES)
    w["w_gate"], w["w_up"], w["w_down"] = w_gate[l], w_up[l], w_down[l]
    w["b_gate"] = b_gate[l].reshape(N_EXPERTS, 1, D_FF)
    w["b_up"] = b_up[l].reshape(N_EXPERTS, 1, D_FF)
    w["b_down"] = b_down[l].reshape(N_EXPERTS, 1, D_MODEL)
    w["lambda_init"] = 0.8 - 0.6 * math.exp(-0.3 * l)
    return w


def _layer(x2d, B, S, w, g_final, *, final):
    T = x2d.shape[0]
    q, k, vt, dq, dk, dvt = _proj_call(x2d, S, w)
    o_mla = _attn_call(q, k, vt, B, S, mla=True)
    o_diff = _attn_call(dq, dk, dvt, B, S, mla=False, lam=w["lam"], g=w["g_subln"],
                        lambda_init=w["lambda_init"])
    x2, hf, ri, rg, cnt = _post_call(x2d, o_mla, o_diff, w)

    counts = cnt[0, :N_EXPERTS].astype(I32)
    padded = ((counts + MOE_TILE - 1) // MOE_TILE) * MOE_TILE
    pad_end = jnp.cumsum(padded)
    pad_start = pad_end - padded
    idx = ri[:, :TOP_K]
    dest = pad_start[idx] + ri[:, TOP_K:2 * TOP_K]
    dest_t = dest.T
    n_blocks = (T * TOP_K) // MOE_TILE + N_EXPERTS
    n_slots = n_blocks * MOE_TILE
    blk_start = jnp.arange(n_blocks, dtype=I32) * MOE_TILE
    block_e = jnp.minimum(jnp.sum((pad_end[None, :] <= blk_start[:, None]).astype(I32), axis=1),
                          N_EXPERTS - 1).astype(I32)
    tok_ids = jnp.broadcast_to(jnp.arange(T, dtype=I32)[:, None], (T, TOP_K))
    slot_tok = jnp.zeros((n_slots,), I32).at[dest.reshape(-1)].set(
        tok_ids.reshape(-1), unique_indices=True)

    xs = _dispatch_call(slot_tok, hf, n_slots)
    ys = _expert_call(block_e, xs, w)
    return _combine_call(dest_t, rg, x2, g_final, ys, final=final)


def kernel(x_prompt, x_sample, g_attn, w_in, g_cq, w_uq, g_ckv, w_uk, w_uv, lam_q1, lam_k1,
           lam_q2, lam_k2, g_subln, w_out, g_ffn, w_router, b_router, w_gate, b_gate, w_up, b_up,
           w_down, b_down, g_final):
    depth = w_in.shape[0]
    gfin = g_final.reshape(1, D_MODEL)

    def trunk(x):
        B, S, D = x.shape
        assert D == D_MODEL and S % TOK_BLOCK == 0
        x2d = x.reshape(B * S, D)
        for l in range(depth):
            w = _prep_layer(l, S, g_attn, w_in, g_cq, w_uq, g_ckv, w_uk, w_uv, lam_q1, lam_k1,
                            lam_q2, lam_k2, g_subln, w_out, g_ffn, w_router, b_router, w_gate,
                            b_gate, w_up, b_up, w_down, b_down)
            x2d = _layer(x2d, B, S, w, gfin, final=(l == depth - 1))
        return x2d.reshape(B, S, D)

    return (trunk(x_prompt), trunk(x_sample))
```

```python
import functools
import math

import jax
import jax.numpy as jnp
from jax import lax
from jax.experimental import pallas as pl
from jax.experimental.pallas import tpu as pltpu

F32 = jnp.float32
BF16 = jnp.bfloat16
I32 = jnp.int32

D_MODEL = 1024
MLA_HEADS = 8
MLA_NOPE = 64
MLA_ROPE = 32
MLA_V = 64
Q_LORA = 384
KV_LORA = 256
MLA_THETA = 10000.0
DIFF_HEADS = 4
DIFF_HD = 64
DIFF_ROT = DIFF_HD // 4
DIFF_THETA = 500000.0
DIFF_QK = DIFF_HEADS * 2 * DIFF_HD
DIFF_VW = DIFF_HEADS * 2 * DIFF_HD
N_EXPERTS = 32
TOP_K = 4
D_FF = 1024
SWIGLU_ALPHA = 1.702
SWIGLU_LIMIT = 7.0
RMS_EPS = 1e-5

LANES = 128
HEAD_PAD = 128
P_IN_PAD = Q_LORA + KV_LORA + LANES + 2 * DIFF_QK + DIFF_VW
OFF_CKV = Q_LORA
OFF_KPE = Q_LORA + KV_LORA
OFF_DQ = OFF_KPE + LANES
OFF_DK = OFF_DQ + DIFF_QK
OFF_DV = OFF_DK + DIFF_QK
LOG2_E = math.log2(math.e)
MLA_SCALE = LOG2_E / math.sqrt(MLA_NOPE + MLA_ROPE)
DIFF_SCALE = LOG2_E / math.sqrt(DIFF_HD)

TOK_BLOCK = 512
Q_TILE = 512
Q_SUB = 256
MOE_TILE = 512
ROW_TILE = D_MODEL // LANES
DISPATCH_BLOCK = 512
COMBINE_BLOCK = 256
ISSUE_UNROLL = 8
WAIT_ROWS = 128
SCORE_LOOKAHEAD = 6
ONES_ROWS = 16
NEG_BIG = -1e30
VMEM_LIMIT = 56 * 1024 * 1024


def _rms(x, g):
    ms = jnp.mean(x * x, axis=-1, keepdims=True)
    return x * lax.rsqrt(ms + RMS_EPS) * g


def _cparams(sem):
    return pltpu.CompilerParams(dimension_semantics=sem, vmem_limit_bytes=VMEM_LIMIT)


def _proj_kernel(x_ref, ga_ref, win_ref, gcq_ref, wuq_ref, gckv_ref, wuk_ref, wuv_ref,
                 cm_ref, sm_ref, cd_ref, sd_ref,
                 q_ref, k_ref, vt_ref, dq_ref, dk_ref, dvt_ref):
    x = x_ref[...]
    tm = x.shape[0]
    h = _rms(x, ga_ref[...]).astype(BF16)
    proj = jnp.dot(h, win_ref[...], preferred_element_type=F32)

    lane = lax.broadcasted_iota(I32, (tm, LANES), 1)
    first_m = (lane & (MLA_ROPE // 2)) == 0
    first_d = (lane & (DIFF_ROT // 2)) == 0
    cm, sm, cd, sd = cm_ref[...], sm_ref[...], cd_ref[...], sd_ref[...]

    def rope_m(v):
        partner = jnp.where(first_m, pltpu.roll(v, LANES - MLA_ROPE // 2, 1),
                            pltpu.roll(v, MLA_ROPE // 2, 1))
        return v * cm + partner * sm

    def rope_d(v):
        partner = jnp.where(first_d, pltpu.roll(v, LANES - DIFF_ROT // 2, 1),
                            pltpu.roll(v, DIFF_ROT // 2, 1))
        return v * cd + partner * sd

    cq = _rms(proj[:, 0:Q_LORA], gcq_ref[...]).astype(BF16)
    q = jnp.dot(cq, wuq_ref[...], preferred_element_type=F32)
    ckv = _rms(proj[:, OFF_CKV:OFF_KPE], gckv_ref[...]).astype(BF16)
    kn = jnp.dot(ckv, wuk_ref[...], preferred_element_type=F32)
    kpe = rope_m(proj[:, OFF_KPE:OFF_DQ])
    for hh in range(MLA_HEADS):
        sl = slice(HEAD_PAD * hh, HEAD_PAD * (hh + 1))
        q_ref[:, sl] = (rope_m(q[:, sl]) * MLA_SCALE).astype(BF16)
        k_ref[:, sl] = (kn[:, sl] + kpe).astype(BF16)
    v = jnp.dot(ckv, wuv_ref[...], preferred_element_type=F32)
    vt_ref[0] = v.T.astype(BF16)

    for i in range(DIFF_QK // LANES):
        sl = slice(LANES * i, LANES * (i + 1))
        dq_ref[:, sl] = (rope_d(proj[:, OFF_DQ + LANES * i:OFF_DQ + LANES * (i + 1)])
                         * DIFF_SCALE).astype(BF16)
        dk_ref[:, sl] = rope_d(proj[:, OFF_DK + LANES * i:OFF_DK + LANES * (i + 1)]).astype(BF16)
    dvt_ref[0] = proj[:, OFF_DV:OFF_DV + DIFF_VW].T.astype(BF16)


def _proj_call(x2d, S, w):
    T = x2d.shape[0]
    tm = TOK_BLOCK
    nblk = T // tm
    spb = S // tm
    full = lambda shp: pl.BlockSpec(shp, lambda i: (0,) * len(shp))
    tab = pl.BlockSpec((tm, LANES), lambda i: (i % spb, 0))
    tokb = lambda n: pl.BlockSpec((tm, n), lambda i: (i, 0))
    vtb = pl.BlockSpec((1, DIFF_VW, tm), lambda i: (i, 0, 0))
    return pl.pallas_call(
        _proj_kernel,
        grid=(nblk,),
        in_specs=[tokb(D_MODEL), full((1, D_MODEL)), full((D_MODEL, P_IN_PAD)),
                  full((1, Q_LORA)), full((Q_LORA, MLA_HEADS * HEAD_PAD)),
                  full((1, KV_LORA)), full((KV_LORA, MLA_HEADS * HEAD_PAD)),
                  full((KV_LORA, MLA_HEADS * MLA_V)), tab, tab, tab, tab],
        out_specs=[tokb(MLA_HEADS * HEAD_PAD), tokb(MLA_HEADS * HEAD_PAD), vtb,
                   tokb(DIFF_QK), tokb(DIFF_QK), vtb],
        out_shape=[jax.ShapeDtypeStruct((T, MLA_HEADS * HEAD_PAD), BF16),
                   jax.ShapeDtypeStruct((T, MLA_HEADS * HEAD_PAD), BF16),
                   jax.ShapeDtypeStruct((nblk, MLA_HEADS * MLA_V, tm), BF16),
                   jax.ShapeDtypeStruct((T, DIFF_QK), BF16),
                   jax.ShapeDtypeStruct((T, DIFF_QK), BF16),
                   jax.ShapeDtypeStruct((nblk, DIFF_VW, tm), BF16)],
        compiler_params=_cparams(("arbitrary",)),
        name="proj",
    )(x2d, w["g_attn"], w["w_in"], w["g_cq"], w["w_uq"], w["g_ckv"], w["w_uk"], w["w_uv"],
      w["cos_m"], w["sin_m"], w["cos_d"], w["sin_d"])


def _attn_kernel(*refs, mla, n_chunks, kc, lambda_init):
    if mla:
        q_ref, k_ref, vt_ref, o_ref = refs
    else:
        lam_ref, g_ref, q_ref, k_ref, vt_ref, o_ref = refs
    dv = MLA_V if mla else 2 * DIFF_HD
    ones = jnp.ones((ONES_ROWS, kc), BF16)
    if not mla:
        a = lam_ref[...]
        s1 = jnp.sum(a[0:1, :] * a[1:2, :], axis=1, keepdims=True)
        s2 = jnp.sum(a[2:3, :] * a[3:4, :], axis=1, keepdims=True)
        lam = jnp.exp(s1) - jnp.exp(s2) + lambda_init

    def query(t, m):
        rows = slice(Q_SUB * t, Q_SUB * (t + 1))
        if mla:
            return q_ref[rows, HEAD_PAD * m:HEAD_PAD * (m + 1)]
        lane = lax.broadcasted_iota(I32, (Q_SUB, LANES), 1)
        keep = (lane < DIFF_HD) if m == 0 else (lane >= DIFF_HD)
        qf = q_ref[rows, :]
        return jnp.where(keep, qf, jnp.zeros_like(qf))

    def scores(t, c, m):
        if mla:
            kb = k_ref[kc * c:kc * (c + 1), HEAD_PAD * m:HEAD_PAD * (m + 1)]
        else:
            kb = k_ref[kc * c:kc * (c + 1), :]
        return lax.dot_general(kb, qs[t][m], (((1,), (1,)), ((), ())),
                               preferred_element_type=F32)

    def finish(t, acc):
        outs = [acc[m][0:dv] / acc[m][dv:dv + 1] for m in range(2)]
        if mla:
            o = jnp.concatenate(outs, axis=0)
        else:
            o = outs[0] - lam * outs[1]
            ms = jnp.mean(o * o, axis=0, keepdims=True)
            o = o * lax.rsqrt(ms + RMS_EPS) * g_ref[...] * (1.0 - lambda_init)
        o_ref[Q_SUB * t:Q_SUB * (t + 1), :] = o.T.astype(BF16)

    n_sub = q_ref.shape[0] // Q_SUB
    qs = [[query(t, m) for m in range(2)] for t in range(n_sub)]
    order = [(t, c, m) for t in range(n_sub) for c in range(n_chunks) for m in range(2)]
    pending = [scores(*order[i]) for i in range(min(SCORE_LOOKAHEAD, len(order)))]
    m_i = acc = None
    for i, (t, c, m) in enumerate(order):
        if c == 0 and m == 0:
            m_i = [jnp.full((1, Q_SUB), -jnp.inf, F32) for _ in range(2)]
            acc = [jnp.zeros((dv + ONES_ROWS, Q_SUB), F32) for _ in range(2)]
        s = pending.pop(0)
        if i + SCORE_LOOKAHEAD < len(order):
            pending.append(scores(*order[i + SCORE_LOOKAHEAD]))
        vb = vt_ref[c, MLA_V * m:MLA_V * (m + 1), :] if mla else vt_ref[c]
        vb = jnp.concatenate([vb, ones], axis=0)
        m_new = jnp.maximum(m_i[m], jnp.max(s, axis=0, keepdims=True))
        alpha = jnp.exp2(m_i[m] - m_new)
        p = jnp.exp2(s - m_new).astype(BF16)
        acc[m] = alpha * acc[m] + jnp.dot(vb, p, preferred_element_type=F32)
        m_i[m] = m_new
        if c == n_chunks - 1 and m == 1:
            finish(t, acc)


def _attn_call(q, k, vt, B, S, *, mla, lam=None, g=None, lambda_init=0.0):
    T = q.shape[0]
    tq = Q_TILE
    nq = S // tq
    kc = vt.shape[2]
    n_chunks = S // kc
    qw = 2 * HEAD_PAD if mla else LANES
    in_specs = [pl.BlockSpec((tq, qw), lambda b, j, i: (b * nq + i, j)),
                pl.BlockSpec((S, qw), lambda b, j, i: (b, j)),
                pl.BlockSpec((n_chunks, LANES, kc), lambda b, j, i: (b, j, 0))]
    args = [q, k, vt]
    if not mla:
        in_specs = [pl.BlockSpec((8, LANES), lambda b, j, i: (0, 0)),
                    pl.BlockSpec((2 * DIFF_HD, 1), lambda b, j, i: (0, 0))] + in_specs
        args = [lam, g] + args
    return pl.pallas_call(
        functools.partial(_attn_kernel, mla=mla, n_chunks=n_chunks, kc=kc,
                          lambda_init=lambda_init),
        grid=(B, 4, nq),
        in_specs=in_specs,
        out_specs=pl.BlockSpec((tq, LANES), lambda b, j, i: (b * nq + i, j)),
        out_shape=jax.ShapeDtypeStruct((T, 4 * LANES), BF16),
        compiler_params=_cparams(("arbitrary", "arbitrary", "arbitrary")),
        name="attn_mla" if mla else "attn_diff",
    )(*args)


def _post_kernel(x_ref, om_ref, od_ref, woa_ref, wob_ref, gf_ref, wrh_ref, wrl_ref, br_ref,
                 x2_ref, hf_ref, ri_ref, rg_ref, cnt_ref, base_ref):
    step = pl.program_id(0)

    @pl.when(step == 0)
    def _():
        base_ref[...] = jnp.zeros_like(base_ref)

    x2 = (x_ref[...]
          + jnp.dot(om_ref[...], woa_ref[...], preferred_element_type=F32)
          + jnp.dot(od_ref[...], wob_ref[...], preferred_element_type=F32))
    x2_ref[...] = x2
    hf = _rms(x2, gf_ref[...])
    _store_rows(hf_ref, hf)
    tm = hf.shape[0]

    h_hi = hf.astype(BF16)
    h_lo = (hf - h_hi.astype(F32)).astype(BF16)
    logits = (jnp.dot(h_hi, wrh_ref[...], preferred_element_type=F32)
              + jnp.dot(h_lo, wrh_ref[...], preferred_element_type=F32)
              + jnp.dot(h_hi, wrl_ref[...], preferred_element_type=F32)
              + br_ref[...])

    lane = lax.broadcasted_iota(I32, (tm, LANES), 1).astype(F32)
    work = logits
    sels, vals, idxs = [], [], []
    for _ in range(TOP_K):
        mx = jnp.max(work, axis=1, keepdims=True)
        idx = jnp.min(jnp.where(work == mx, lane, float(LANES)), axis=1, keepdims=True)
        sel = lane == idx
        work = jnp.where(sel, -jnp.inf, work)
        sels.append(sel)
        vals.append(mx)
        idxs.append(idx)
    es = [jnp.exp(v - vals[0]) for v in vals]
    den = es[0] + es[1] + es[2] + es[3]
    gates = [e / den for e in es]

    multi = jnp.zeros((tm, LANES), F32)
    for sel in sels:
        multi = multi + jnp.where(sel, 1.0, 0.0)
    row = lax.broadcasted_iota(I32, (tm, tm), 0)
    col = lax.broadcasted_iota(I32, (tm, tm), 1)
    lower = jnp.where(row > col, 1.0, 0.0).astype(BF16)
    prefix = jnp.dot(lower, multi.astype(BF16), preferred_element_type=F32) + base_ref[0:1, :]
    ri = jnp.zeros((tm, LANES), F32)
    rg = jnp.zeros((tm, LANES), F32)
    for kk in range(TOP_K):
        rank = jnp.sum(jnp.where(sels[kk], prefix, 0.0), axis=1, keepdims=True)
        ri = jnp.where(lane == float(kk), idxs[kk], ri)
        ri = jnp.where(lane == float(TOP_K + kk), rank, ri)
        rg = jnp.where(lane == float(kk), gates[kk], rg)
    ri_ref[...] = ri.astype(I32)
    rg_ref[...] = rg
    new_base = base_ref[0:1, :] + jnp.sum(multi, axis=0, keepdims=True)
    base_ref[0:1, :] = new_base
    cnt_ref[...] = jnp.broadcast_to(new_base, cnt_ref.shape)


def _post_call(x2d, o_mla, o_diff, w):
    T = x2d.shape[0]
    tm = TOK_BLOCK
    full = lambda shp: pl.BlockSpec(shp, lambda i: (0,) * len(shp))
    tokb = lambda n: pl.BlockSpec((tm, n), lambda i: (i, 0))
    return pl.pallas_call(
        _post_kernel,
        grid=(T // tm,),
        in_specs=[tokb(D_MODEL), tokb(4 * LANES), tokb(4 * LANES),
                  full((4 * LANES, D_MODEL)), full((4 * LANES, D_MODEL)), full((1, D_MODEL)),
                  full((D_MODEL, LANES)), full((D_MODEL, LANES)), full((1, LANES))],
        out_specs=[tokb(D_MODEL), pl.BlockSpec((tm * ROW_TILE, LANES), lambda i: (i, 0)),
                   tokb(LANES), tokb(LANES), full((8, LANES))],
        out_shape=[jax.ShapeDtypeStruct((T, D_MODEL), F32),
                   jax.ShapeDtypeStruct((T * ROW_TILE, LANES), F32),
                   jax.ShapeDtypeStruct((T, LANES), I32),
                   jax.ShapeDtypeStruct((T, LANES), F32),
                   jax.ShapeDtypeStruct((8, LANES), F32)],
        scratch_shapes=[pltpu.VMEM((8, LANES), F32)],
        compiler_params=_cparams(("arbitrary",)),
        name="post",
    )(x2d, o_mla, o_diff, w["w_out_a"], w["w_out_b"], w["g_ffn"],
      w["w_router_hi"], w["w_router_lo"], w["b_router"])


def _store_rows(ref, val):
    n = val.shape[0]
    for s in range(ROW_TILE):
        ref[pl.ds(s, n, stride=ROW_TILE), :] = val[:, LANES * s:LANES * (s + 1)]


def _load_rows(ref, n):
    return jnp.concatenate([ref[pl.ds(s, n, stride=ROW_TILE), :] for s in range(ROW_TILE)],
                           axis=1)


def _row_copy(src, si, dst, di, sem):
    return pltpu.make_async_copy(
        src.at[pl.ds(pl.multiple_of(si * ROW_TILE, ROW_TILE), ROW_TILE), :],
        dst.at[pl.ds(pl.multiple_of(di * ROW_TILE, ROW_TILE), ROW_TILE), :], sem)


def _wait_rows(src, dst, n_rows, sem):
    for _ in range(0, n_rows, WAIT_ROWS):
        pltpu.make_async_copy(src.at[pl.ds(0, WAIT_ROWS * ROW_TILE), :],
                              dst.at[pl.ds(0, WAIT_ROWS * ROW_TILE), :], sem).wait()


def _dispatch_kernel(pend_ref, npad_ref, dest_ref, hf_ref, xs_ref, zero_ref, sem, zsem):
    tb = dest_ref.shape[1]

    @pl.when(pl.program_id(0) == 0)
    def _():
        zero_ref[...] = jnp.zeros_like(zero_ref)

        def zcopy(e):
            start = pl.multiple_of((pend_ref[e] - MOE_TILE) * ROW_TILE, MOE_TILE * ROW_TILE)
            return pltpu.make_async_copy(
                zero_ref, xs_ref.at[pl.ds(start, MOE_TILE * ROW_TILE), :], zsem)

        def zstart(e, c):
            @pl.when(npad_ref[e] > 0)
            def _():
                zcopy(e).start()
            return c

        def zwait(e, c):
            @pl.when(npad_ref[e] > 0)
            def _():
                zcopy(e).wait()
            return c

        lax.fori_loop(0, N_EXPERTS, zstart, 0)
        lax.fori_loop(0, N_EXPERTS, zwait, 0)

        def tcopy(j):
            start = pl.multiple_of(j * (MOE_TILE * ROW_TILE), MOE_TILE * ROW_TILE)
            return pltpu.make_async_copy(
                zero_ref, xs_ref.at[pl.ds(start, MOE_TILE * ROW_TILE), :], zsem)

        def tstart(j, c):
            tcopy(j).start()
            return c

        def twait(j, c):
            tcopy(j).wait()
            return c

        first_free = pend_ref[N_EXPERTS - 1] // MOE_TILE
        n_blocks = xs_ref.shape[0] // (MOE_TILE * ROW_TILE)
        lax.fori_loop(first_free, n_blocks, tstart, 0)
        lax.fori_loop(first_free, n_blocks, twait, 0)

    def issue(j, c):
        base = pl.multiple_of(j * ISSUE_UNROLL, ISSUE_UNROLL)
        for u in range(ISSUE_UNROLL):
            for kk in range(TOP_K):
                _row_copy(hf_ref, base + u, xs_ref, dest_ref[kk, base + u], sem).start()
        return c

    lax.fori_loop(0, tb // ISSUE_UNROLL, issue, 0)
    _wait_rows(hf_ref, xs_ref, TOP_K * tb, sem)


def _dispatch_call(pad_end, n_pad, dest_t, hf, n_slots):
    T = hf.shape[0] // ROW_TILE
    tb = DISPATCH_BLOCK
    return pl.pallas_call(
        _dispatch_kernel,
        grid_spec=pltpu.PrefetchScalarGridSpec(
            num_scalar_prefetch=2,
            grid=(T // tb,),
            in_specs=[pl.BlockSpec((TOP_K, tb), lambda i, pe, npd: (0, i),
                                   memory_space=pltpu.SMEM),
                      pl.BlockSpec((tb * ROW_TILE, LANES), lambda i, pe, npd: (i, 0))],
            out_specs=pl.BlockSpec(memory_space=pl.ANY),
            scratch_shapes=[pltpu.VMEM((MOE_TILE * ROW_TILE, LANES), F32),
                            pltpu.SemaphoreType.DMA, pltpu.SemaphoreType.DMA]),
        out_shape=jax.ShapeDtypeStruct((n_slots * ROW_TILE, LANES), F32),
        compiler_params=_cparams(("arbitrary",)),
        name="dispatch",
    )(pad_end, n_pad, dest_t, hf)


def _expert_kernel(be_ref, xs_ref, wg_ref, bg_ref, wu_ref, bu_ref, wd_ref, bd_ref, ys_ref,
                   wgb, wub, wdb):
    i = pl.program_id(0)
    prev = be_ref[jnp.maximum(i - 1, 0)]

    @pl.when(jnp.logical_or(i == 0, be_ref[i] != prev))
    def _():
        wgb[...] = wg_ref[0].astype(BF16)
        wub[...] = wu_ref[0].astype(BF16)
        wdb[...] = wd_ref[0].astype(BF16)

    x = _load_rows(xs_ref, MOE_TILE).astype(BF16)
    g = jnp.dot(x, wgb[...], preferred_element_type=F32) + bg_ref[0]
    u = jnp.dot(x, wub[...], preferred_element_type=F32) + bu_ref[0]
    g = jnp.minimum(g, SWIGLU_LIMIT)
    u = jnp.clip(u, -SWIGLU_LIMIT, SWIGLU_LIMIT)
    a = g * (1.0 / (1.0 + jnp.exp(-SWIGLU_ALPHA * g))) * (u + 1.0)
    y = jnp.dot(a.astype(BF16), wdb[...], preferred_element_type=F32) + bd_ref[0]
    _store_rows(ys_ref, y)


def _expert_call(block_e, xs, w):
    n_slots = xs.shape[0] // ROW_TILE
    wspec = lambda r, c: pl.BlockSpec((1, r, c), lambda i, be: (be[i], 0, 0))
    slotb = pl.BlockSpec((MOE_TILE * ROW_TILE, LANES), lambda i, be: (i, 0))
    return pl.pallas_call(
        _expert_kernel,
        grid_spec=pltpu.PrefetchScalarGridSpec(
            num_scalar_prefetch=1,
            grid=(n_slots // MOE_TILE,),
            in_specs=[slotb, wspec(D_MODEL, D_FF), wspec(1, D_FF), wspec(D_MODEL, D_FF),
                      wspec(1, D_FF), wspec(D_FF, D_MODEL), wspec(1, D_MODEL)],
            out_specs=slotb,
            scratch_shapes=[pltpu.VMEM((D_MODEL, D_FF), BF16), pltpu.VMEM((D_MODEL, D_FF), BF16),
                            pltpu.VMEM((D_FF, D_MODEL), BF16)]),
        out_shape=jax.ShapeDtypeStruct((n_slots * ROW_TILE, LANES), F32),
        compiler_params=_cparams(("arbitrary",)),
        name="experts",
    )(block_e, xs, w["w_gate"], w["b_gate"], w["w_up"], w["b_up"], w["w_down"], w["b_down"])


def _combine_kernel(dest_ref, rg_ref, x2_ref, gfin_ref, ys_ref, out_ref, buf, sem, *, final):
    tb = x2_ref.shape[0]

    def issue(j, c):
        base = pl.multiple_of(j * ISSUE_UNROLL, ISSUE_UNROLL)
        for u in range(ISSUE_UNROLL):
            for kk in range(TOP_K):
                _row_copy(ys_ref, dest_ref[kk, base + u], buf.at[kk], base + u, sem).start()
        return c

    lax.fori_loop(0, tb // ISSUE_UNROLL, issue, 0)
    _wait_rows(ys_ref, buf.at[0], TOP_K * tb, sem)

    rg = rg_ref[...]
    y = x2_ref[...]
    for kk in range(TOP_K):
        y = y + _load_rows(buf.at[kk], tb) * rg[:, kk:kk + 1]
    if final:
        y = _rms(y, gfin_ref[...])
    out_ref[...] = y


def _combine_call(dest_t, rg, x2, g_final, ys, *, final):
    T = x2.shape[0]
    tb = COMBINE_BLOCK
    return pl.pallas_call(
        functools.partial(_combine_kernel, final=final),
        grid=(T // tb,),
        in_specs=[pl.BlockSpec((TOP_K, tb), lambda i: (0, i), memory_space=pltpu.SMEM),
                  pl.BlockSpec((tb, LANES), lambda i: (i, 0)),
                  pl.BlockSpec((tb, D_MODEL), lambda i: (i, 0)),
                  pl.BlockSpec((1, D_MODEL), lambda i: (0, 0)),
                  pl.BlockSpec(memory_space=pl.ANY)],
        out_specs=pl.BlockSpec((tb, D_MODEL), lambda i: (i, 0)),
        out_shape=jax.ShapeDtypeStruct((T, D_MODEL), F32),
        scratch_shapes=[pltpu.VMEM((TOP_K, tb * ROW_TILE, LANES), F32),
                        pltpu.SemaphoreType.DMA],
        compiler_params=_cparams(("arbitrary",)),
        name="combine",
    )(dest_t, rg, x2, g_final, ys)


def _rope_tables(S, rot_dim, theta, group, lead):
    half = rot_dim // 2
    inv_freq = jnp.power(theta, -2.0 * jnp.arange(half, dtype=F32) / rot_dim)
    ang = jnp.arange(S, dtype=F32)[:, None] * inv_freq[None, :]
    cos = jnp.cos(ang)
    sin = jnp.sin(ang)
    rest = group - lead - rot_dim
    cg = jnp.concatenate([jnp.ones((S, lead), F32), cos, cos, jnp.ones((S, rest), F32)], axis=1)
    sg = jnp.concatenate([jnp.zeros((S, lead), F32), -sin, sin, jnp.zeros((S, rest), F32)], axis=1)
    reps = LANES // group
    return jnp.tile(cg, (1, reps)), jnp.tile(sg, (1, reps))


def _prep_layer(l, S, g_attn, w_in, g_cq, w_uq, g_ckv, w_uk, w_uv, lam_q1, lam_k1, lam_q2,
                lam_k2, g_subln, w_out, g_ffn, w_router, b_router, w_gate, b_gate, w_up, b_up,
                w_down, b_down):
    w = {}
    wi = w_in[l]
    kpe_cols = jnp.zeros((D_MODEL, LANES), F32).at[:, MLA_NOPE:MLA_NOPE + MLA_ROPE].set(
        wi[:, OFF_KPE:OFF_KPE + MLA_ROPE])
    w["w_in"] = jnp.concatenate(
        [wi[:, :OFF_KPE], kpe_cols, wi[:, OFF_KPE + MLA_ROPE:]], axis=1).astype(BF16)
    dqk = MLA_NOPE + MLA_ROPE
    wq = w_uq[l].reshape(Q_LORA, MLA_HEADS, dqk)
    w["w_uq"] = jnp.pad(wq, ((0, 0), (0, 0), (0, HEAD_PAD - dqk))).reshape(
        Q_LORA, MLA_HEADS * HEAD_PAD).astype(BF16)
    wk = w_uk[l].reshape(KV_LORA, MLA_HEADS, MLA_NOPE)
    w["w_uk"] = jnp.pad(wk, ((0, 0), (0, 0), (0, HEAD_PAD - MLA_NOPE))).reshape(
        KV_LORA, MLA_HEADS * HEAD_PAD).astype(BF16)
    w["w_uv"] = w_uv[l].astype(BF16)
    w["g_attn"] = g_attn[l].reshape(1, D_MODEL)
    w["g_cq"] = g_cq[l].reshape(1, Q_LORA)
    w["g_ckv"] = g_ckv[l].reshape(1, KV_LORA)
    w["cos_m"], w["sin_m"] = _rope_tables(S, MLA_ROPE, MLA_THETA, LANES, MLA_NOPE)
    w["cos_d"], w["sin_d"] = _rope_tables(S, DIFF_ROT, DIFF_THETA, DIFF_HD, 0)
    lam = jnp.stack([lam_q1[l], lam_k1[l], lam_q2[l], lam_k2[l]]).astype(F32)
    w["lam"] = jnp.pad(lam, ((0, 8 - 4), (0, LANES - DIFF_HD)))
    w["g_subln"] = g_subln[l].reshape(2 * DIFF_HD, 1)
    wo = w_out[l].astype(BF16)
    w["w_out_a"] = wo[:MLA_HEADS * MLA_V]
    w["w_out_b"] = wo[MLA_HEADS * MLA_V:]
    w["g_ffn"] = g_ffn[l].reshape(1, D_MODEL)
    wr = jnp.pad(w_router[l], ((0, 0), (0, LANES - N_EXPERTS)))
    w["w_router_hi"] = wr.astype(BF16)
    w["w_router_lo"] = (wr - w["w_router_hi"].astype(F32)).astype(BF16)
    w["b_router"] = jnp.pad(b_router[l].astype(F32), (0, LANES - N_EXPERTS),
                            constant_values=NEG_BIG).reshape(1, LANES)
    w["w_gate"], w["w_up"], w["w_down"] = w_gate[l], w_up[l], w_down[l]
    w["b_gate"] = b_gate[l].reshape(N_EXPERTS, 1, D_FF)
    w["b_up"] = b_up[l].reshape(N_EXPERTS, 1, D_FF)
    w["b_down"] = b_down[l].reshape(N_EXPERTS, 1, D_MODEL)
    w["lambda_init"] = 0.8 - 0.6 * math.exp(-0.3 * l)
    return w


def _layer(x2d, B, S, w, g_final, *, final):
    T = x2d.shape[0]
    q, k, vt, dq, dk, dvt = _proj_call(x2d, S, w)
    o_mla = _attn_call(q, k, vt, B, S, mla=True)
    o_diff = _attn_call(dq, dk, dvt, B, S, mla=False, lam=w["lam"], g=w["g_subln"],
                        lambda_init=w["lambda_init"])
    x2, hf, ri, rg, cnt = _post_call(x2d, o_mla, o_diff, w)

    counts = cnt[0, :N_EXPERTS].astype(I32)
    padded = ((counts + MOE_TILE - 1) // MOE_TILE) * MOE_TILE
    pad_end = jnp.cumsum(padded)
    pad_start = pad_end - padded
    idx = ri[:, :TOP_K]
    dest = pad_start[idx] + ri[:, TOP_K:2 * TOP_K]
    dest_t = dest.T
    n_blocks = (T * TOP_K) // MOE_TILE + N_EXPERTS
    n_slots = n_blocks * MOE_TILE
    blk_start = jnp.arange(n_blocks, dtype=I32) * MOE_TILE
    block_e = jnp.minimum(jnp.sum((pad_end[None, :] <= blk_start[:, None]).astype(I32), axis=1),
                          N_EXPERTS - 1).astype(I32)
    xs = _dispatch_call(pad_end.astype(I32), (padded - counts).astype(I32), dest_t, hf, n_slots)
    ys = _expert_call(block_e, xs, w)
    return _combine_call(dest_t, rg, x2, g_final, ys, final=final)


def kernel(x_prompt, x_sample, g_attn, w_in, g_cq, w_uq, g_ckv, w_uk, w_uv, lam_q1, lam_k1,
           lam_q2, lam_k2, g_subln, w_out, g_ffn, w_router, b_router, w_gate, b_gate, w_up, b_up,
           w_down, b_down, g_final):
    depth = w_in.shape[0]
    gfin = g_final.reshape(1, D_MODEL)

    def trunk(x):
        B, S, D = x.shape
        assert D == D_MODEL and S % TOK_BLOCK == 0
        x2d = x.reshape(B * S, D)
        for l in range(depth):
            w = _prep_layer(l, S, g_attn, w_in, g_cq, w_uq, g_ckv, w_uk, w_uv, lam_q1, lam_k1,
                            lam_q2, lam_k2, g_subln, w_out, g_ffn, w_router, b_router, w_gate,
                            b_gate, w_up, b_up, w_down, b_down)
            x2d = _layer(x2d, B, S, w, gfin, final=(l == depth - 1))
        return x2d.reshape(B, S, D)

    return (trunk(x_prompt), trunk(x_sample))
```

```python
import functools
import math

import jax
import jax.numpy as jnp
from jax import lax
from jax.experimental import pallas as pl
from jax.experimental.pallas import tpu as pltpu

F32 = jnp.float32
BF16 = jnp.bfloat16
I32 = jnp.int32

D_MODEL = 1024
MLA_HEADS = 8
MLA_NOPE = 64
MLA_ROPE = 32
MLA_V = 64
Q_LORA = 384
KV_LORA = 256
MLA_THETA = 10000.0
DIFF_HEADS = 4
DIFF_HD = 64
DIFF_ROT = DIFF_HD // 4
DIFF_THETA = 500000.0
DIFF_QK = DIFF_HEADS * 2 * DIFF_HD
DIFF_VW = DIFF_HEADS * 2 * DIFF_HD
N_EXPERTS = 32
TOP_K = 4
D_FF = 1024
SWIGLU_ALPHA = 1.702
SWIGLU_LIMIT = 7.0
RMS_EPS = 1e-5

LANES = 128
HEAD_PAD = 128
P_IN_PAD = Q_LORA + KV_LORA + LANES + 2 * DIFF_QK
OFF_CKV = Q_LORA
OFF_KPE = Q_LORA + KV_LORA
OFF_DQ = OFF_KPE + LANES
OFF_DK = OFF_DQ + DIFF_QK
LOG2_E = math.log2(math.e)
MLA_SCALE = LOG2_E / math.sqrt(MLA_NOPE + MLA_ROPE)
DIFF_SCALE = LOG2_E / math.sqrt(DIFF_HD)

TOK_BLOCK = 512
Q_TILE = 1024
Q_SUB = 256
KEY_CHUNK = 256
SUB_INTERLEAVE = 1
MOE_TILE = 512
EXPERT_GROUPS = 1
ROW_TILE = D_MODEL // LANES
DISPATCH_BLOCK = 512
COMBINE_BLOCK = 256
ISSUE_UNROLL = 8
DMA_THREADS = 2
WAIT_ROWS = 128
SCORE_LOOKAHEAD = 8
ONES_ROWS = 16
NEG_BIG = -1e30
VMEM_LIMIT = 56 * 1024 * 1024


def _rms(x, g):
    ms = jnp.mean(x * x, axis=-1, keepdims=True)
    return x * lax.rsqrt(ms + RMS_EPS) * g


def _cparams(sem):
    return pltpu.CompilerParams(dimension_semantics=sem, vmem_limit_bytes=VMEM_LIMIT)


def _proj_kernel(x_ref, ga_ref, win_ref, wdv_ref, gcq_ref, wuq_ref, gckv_ref, wuk_ref, wuv_ref,
                 cm_ref, sm_ref, cd_ref, sd_ref,
                 q_ref, k_ref, vt_ref, dq_ref, dk_ref, dvt_ref):
    x = x_ref[...]
    tm = x.shape[0]
    h = _rms(x, ga_ref[...]).astype(BF16)
    proj = jnp.dot(h, win_ref[...], preferred_element_type=F32)
    ckv = _rms(proj[:, OFF_CKV:OFF_KPE], gckv_ref[...]).astype(BF16)

    lane = lax.broadcasted_iota(I32, (tm, LANES), 1)
    first_m = (lane & (MLA_ROPE // 2)) == 0
    first_d = (lane & (DIFF_ROT // 2)) == 0
    cm, sm, cd, sd = cm_ref[...], sm_ref[...], cd_ref[...], sd_ref[...]

    def rope_m(v):
        partner = jnp.where(first_m, pltpu.roll(v, LANES - MLA_ROPE // 2, 1),
                            pltpu.roll(v, MLA_ROPE // 2, 1))
        return v * cm + partner * sm

    def rope_d(v):
        partner = jnp.where(first_d, pltpu.roll(v, LANES - DIFF_ROT // 2, 1),
                            pltpu.roll(v, DIFF_ROT // 2, 1))
        return v * cd + partner * sd

    cq = _rms(proj[:, 0:Q_LORA], gcq_ref[...]).astype(BF16)
    q = jnp.dot(cq, wuq_ref[...], preferred_element_type=F32)
    kn = jnp.dot(ckv, wuk_ref[...], preferred_element_type=F32)
    kpe = rope_m(proj[:, OFF_KPE:OFF_DQ])
    for hh in range(MLA_HEADS):
        sl = slice(HEAD_PAD * hh, HEAD_PAD * (hh + 1))
        q_ref[:, sl] = (rope_m(q[:, sl]) * MLA_SCALE).astype(BF16)
        k_ref[:, sl] = (kn[:, sl] + kpe).astype(BF16)

    for i in range(DIFF_QK // LANES):
        sl = slice(LANES * i, LANES * (i + 1))
        dq_ref[:, sl] = (rope_d(proj[:, OFF_DQ + LANES * i:OFF_DQ + LANES * (i + 1)])
                         * DIFF_SCALE).astype(BF16)
        dk_ref[:, sl] = rope_d(proj[:, OFF_DK + LANES * i:OFF_DK + LANES * (i + 1)]).astype(BF16)
    vt_ref[0] = jnp.dot(ckv, wuv_ref[...], preferred_element_type=F32).T.astype(BF16)
    dvt_ref[0] = jnp.dot(h, wdv_ref[...], preferred_element_type=F32).T.astype(BF16)


def _proj_call(x2d, S, w):
    T = x2d.shape[0]
    tm = TOK_BLOCK
    nblk = T // tm
    spb = S // tm
    full = lambda shp: pl.BlockSpec(shp, lambda i: (0,) * len(shp))
    tab = pl.BlockSpec((tm, LANES), lambda i: (i % spb, 0))
    tokb = lambda n: pl.BlockSpec((tm, n), lambda i: (i, 0))
    vtb = pl.BlockSpec((1, DIFF_VW, tm), lambda i: (i, 0, 0))
    return pl.pallas_call(
        _proj_kernel,
        grid=(nblk,),
        in_specs=[tokb(D_MODEL), full((1, D_MODEL)), full((D_MODEL, P_IN_PAD)),
                  full((D_MODEL, DIFF_VW)),
                  full((1, Q_LORA)), full((Q_LORA, MLA_HEADS * HEAD_PAD)),
                  full((1, KV_LORA)), full((KV_LORA, MLA_HEADS * HEAD_PAD)),
                  full((KV_LORA, MLA_HEADS * MLA_V)), tab, tab, tab, tab],
        out_specs=[tokb(MLA_HEADS * HEAD_PAD), tokb(MLA_HEADS * HEAD_PAD), vtb,
                   tokb(DIFF_QK), tokb(DIFF_QK), vtb],
        out_shape=[jax.ShapeDtypeStruct((T, MLA_HEADS * HEAD_PAD), BF16),
                   jax.ShapeDtypeStruct((T, MLA_HEADS * HEAD_PAD), BF16),
                   jax.ShapeDtypeStruct((nblk, MLA_HEADS * MLA_V, tm), BF16),
                   jax.ShapeDtypeStruct((T, DIFF_QK), BF16),
                   jax.ShapeDtypeStruct((T, DIFF_QK), BF16),
                   jax.ShapeDtypeStruct((nblk, DIFF_VW, tm), BF16)],
        compiler_params=_cparams(("arbitrary",)),
        name="proj",
    )(x2d, w["g_attn"], w["w_in"], w["w_dv"], w["g_cq"], w["w_uq"], w["g_ckv"], w["w_uk"],
      w["w_uv"], w["cos_m"], w["sin_m"], w["cos_d"], w["sin_d"])


def _attn_kernel(*refs, mla, lambda_init):
    if mla:
        q_ref, k_ref, vt_ref, o_ref = refs
    else:
        lam_ref, g_ref, q_ref, k_ref, vt_ref, o_ref = refs
    dv = MLA_V if mla else 2 * DIFF_HD
    kc = KEY_CHUNK
    vt_blk = vt_ref.shape[2]
    n_chunks = k_ref.shape[0] // kc
    ones = jnp.ones((ONES_ROWS, kc), BF16)
    if not mla:
        a = lam_ref[...]
        s1 = jnp.sum(a[0:1, :] * a[1:2, :], axis=1, keepdims=True)
        s2 = jnp.sum(a[2:3, :] * a[3:4, :], axis=1, keepdims=True)
        lam = jnp.exp(s1) - jnp.exp(s2) + lambda_init

    def query(t, m):
        rows = slice(Q_SUB * t, Q_SUB * (t + 1))
        if mla:
            return q_ref[rows, HEAD_PAD * m:HEAD_PAD * (m + 1)]
        lane = lax.broadcasted_iota(I32, (Q_SUB, LANES), 1)
        keep = (lane < DIFF_HD) if m == 0 else (lane >= DIFF_HD)
        qf = q_ref[rows, :]
        return jnp.where(keep, qf, jnp.zeros_like(qf))

    def scores(t, c, m):
        if mla:
            kb = k_ref[kc * c:kc * (c + 1), HEAD_PAD * m:HEAD_PAD * (m + 1)]
        else:
            kb = k_ref[kc * c:kc * (c + 1), :]
        return lax.dot_general(kb, qs[t][m], (((1,), (1,)), ((), ())),
                               preferred_element_type=F32)

    def finish(t, acc):
        outs = [acc[m][0:dv] / acc[m][dv:dv + 1] for m in range(2)]
        if mla:
            o = jnp.concatenate(outs, axis=0)
        else:
            o = outs[0] - lam * outs[1]
            ms = jnp.mean(o * o, axis=0, keepdims=True)
            o = o * lax.rsqrt(ms + RMS_EPS) * g_ref[...] * (1.0 - lambda_init)
        o_ref[Q_SUB * t:Q_SUB * (t + 1), :] = o.T.astype(BF16)

    n_sub = q_ref.shape[0] // Q_SUB
    qs = [[query(t, m) for m in range(2)] for t in range(n_sub)]
    order = [(t, c, m) for tp in range(0, n_sub, SUB_INTERLEAVE) for c in range(n_chunks)
             for t in range(tp, tp + SUB_INTERLEAVE) for m in range(2)]
    pending = [scores(*order[i]) for i in range(min(SCORE_LOOKAHEAD, len(order)))]
    m_all = [[jnp.full((1, Q_SUB), -jnp.inf, F32) for _ in range(2)] for _ in range(n_sub)]
    acc_all = [[jnp.zeros((dv + ONES_ROWS, Q_SUB), F32) for _ in range(2)] for _ in range(n_sub)]
    for i, (t, c, m) in enumerate(order):
        m_i, acc = m_all[t], acc_all[t]
        s = pending.pop(0)
        if i + SCORE_LOOKAHEAD < len(order):
            pending.append(scores(*order[i + SCORE_LOOKAHEAD]))
        vrows = slice(MLA_V * m, MLA_V * (m + 1)) if mla else slice(0, dv)
        vcols = slice((kc * c) % vt_blk, (kc * c) % vt_blk + kc)
        vb = jnp.concatenate([vt_ref[(kc * c) // vt_blk, vrows, vcols], ones], axis=0)
        m_new = jnp.maximum(m_i[m], jnp.max(s, axis=0, keepdims=True))
        alpha = jnp.exp2(m_i[m] - m_new)
        p = jnp.exp2(s - m_new).astype(BF16)
        acc[m] = alpha * acc[m] + jnp.dot(vb, p, preferred_element_type=F32)
        m_i[m] = m_new
        if c == n_chunks - 1 and m == 1:
            finish(t, acc)


def _attn_call(q, k, vt, B, S, *, mla, lam=None, g=None, lambda_init=0.0):
    T = q.shape[0]
    tq = min(Q_TILE, S)
    nq = S // tq
    vt_blk = vt.shape[2]
    assert vt_blk % KEY_CHUNK == 0 and S % vt_blk == 0
    qw = 2 * HEAD_PAD if mla else LANES
    in_specs = [pl.BlockSpec((tq, qw), lambda b, j, i: (b * nq + i, j)),
                pl.BlockSpec((S, qw), lambda b, j, i: (b, j)),
                pl.BlockSpec((S // vt_blk, LANES, vt_blk), lambda b, j, i: (b, j, 0))]
    args = [q, k, vt]
    if not mla:
        in_specs = [pl.BlockSpec((8, LANES), lambda b, j, i: (0, 0)),
                    pl.BlockSpec((2 * DIFF_HD, 1), lambda b, j, i: (0, 0))] + in_specs
        args = [lam, g] + args
    return pl.pallas_call(
        functools.partial(_attn_kernel, mla=mla, lambda_init=lambda_init),
        grid=(B, 4, nq),
        in_specs=in_specs,
        out_specs=pl.BlockSpec((tq, LANES), lambda b, j, i: (b * nq + i, j)),
        out_shape=jax.ShapeDtypeStruct((T, 4 * LANES), BF16),
        compiler_params=_cparams(("arbitrary", "arbitrary", "arbitrary")),
        name="attn_mla" if mla else "attn_diff",
    )(*args)


def _post_kernel(x_ref, om_ref, od_ref, woa_ref, wob_ref, gf_ref, wrh_ref, wrl_ref, br_ref,
                 x2_ref, hf_ref, ri_ref, rg_ref, cnt_ref, base_ref):
    step = pl.program_id(0)

    @pl.when(step == 0)
    def _():
        base_ref[...] = jnp.zeros_like(base_ref)

    x2 = (x_ref[...]
          + jnp.dot(om_ref[...], woa_ref[...], preferred_element_type=F32)
          + jnp.dot(od_ref[...], wob_ref[...], preferred_element_type=F32))
    x2_ref[...] = x2
    hf = _rms(x2, gf_ref[...])
    _store_rows(hf_ref, hf)
    tm = hf.shape[0]

    h_hi = hf.astype(BF16)
    h_lo = (hf - h_hi.astype(F32)).astype(BF16)
    logits = (jnp.dot(h_hi, wrh_ref[...], preferred_element_type=F32)
              + jnp.dot(h_lo, wrh_ref[...], preferred_element_type=F32)
              + jnp.dot(h_hi, wrl_ref[...], preferred_element_type=F32)
              + br_ref[...])

    lane = lax.broadcasted_iota(I32, (tm, LANES), 1).astype(F32)
    work = logits
    sels, vals, idxs = [], [], []
    for _ in range(TOP_K):
        mx = jnp.max(work, axis=1, keepdims=True)
        idx = jnp.min(jnp.where(work == mx, lane, float(LANES)), axis=1, keepdims=True)
        sel = lane == idx
        work = jnp.where(sel, -jnp.inf, work)
        sels.append(sel)
        vals.append(mx)
        idxs.append(idx)
    es = [jnp.exp(v - vals[0]) for v in vals]
    den = es[0] + es[1] + es[2] + es[3]
    gates = [e / den for e in es]

    multi = jnp.zeros((tm, LANES), F32)
    for sel in sels:
        multi = multi + jnp.where(sel, 1.0, 0.0)
    row = lax.broadcasted_iota(I32, (tm, tm), 0)
    col = lax.broadcasted_iota(I32, (tm, tm), 1)
    lower = jnp.where(row > col, 1.0, 0.0).astype(BF16)
    prefix = jnp.dot(lower, multi.astype(BF16), preferred_element_type=F32) + base_ref[0:1, :]
    ri = jnp.zeros((tm, LANES), F32)
    rg = jnp.zeros((tm, LANES), F32)
    for kk in range(TOP_K):
        rank = jnp.sum(jnp.where(sels[kk], prefix, 0.0), axis=1, keepdims=True)
        ri = jnp.where(lane == float(kk), idxs[kk], ri)
        ri = jnp.where(lane == float(TOP_K + kk), rank, ri)
        rg = jnp.where(lane == float(kk), gates[kk], rg)
    ri_ref[...] = ri.astype(I32)
    rg_ref[...] = rg
    new_base = base_ref[0:1, :] + jnp.sum(multi, axis=0, keepdims=True)
    base_ref[0:1, :] = new_base
    cnt_ref[...] = jnp.broadcast_to(new_base, cnt_ref.shape)


def _post_call(x2d, o_mla, o_diff, w):
    T = x2d.shape[0]
    tm = TOK_BLOCK
    full = lambda shp: pl.BlockSpec(shp, lambda i: (0,) * len(shp))
    tokb = lambda n: pl.BlockSpec((tm, n), lambda i: (i, 0))
    return pl.pallas_call(
        _post_kernel,
        grid=(T // tm,),
        in_specs=[tokb(D_MODEL), tokb(4 * LANES), tokb(4 * LANES),
                  full((4 * LANES, D_MODEL)), full((4 * LANES, D_MODEL)), full((1, D_MODEL)),
                  full((D_MODEL, LANES)), full((D_MODEL, LANES)), full((1, LANES))],
        out_specs=[tokb(D_MODEL), pl.BlockSpec((tm * ROW_TILE, LANES), lambda i: (i, 0)),
                   tokb(LANES), tokb(LANES), full((8, LANES))],
        out_shape=[jax.ShapeDtypeStruct((T, D_MODEL), F32),
                   jax.ShapeDtypeStruct((T * ROW_TILE, LANES), F32),
                   jax.ShapeDtypeStruct((T, LANES), I32),
                   jax.ShapeDtypeStruct((T, LANES), F32),
                   jax.ShapeDtypeStruct((8, LANES), F32)],
        scratch_shapes=[pltpu.VMEM((8, LANES), F32)],
        compiler_params=_cparams(("arbitrary",)),
        name="post",
    )(x2d, o_mla, o_diff, w["w_out_a"], w["w_out_b"], w["g_ffn"],
      w["w_router_hi"], w["w_router_lo"], w["b_router"])


def _store_rows(ref, val):
    n = val.shape[0]
    for s in range(ROW_TILE):
        ref[pl.ds(s, n, stride=ROW_TILE), :] = val[:, LANES * s:LANES * (s + 1)]


def _load_rows(ref, n):
    return jnp.concatenate([ref[pl.ds(s, n, stride=ROW_TILE), :] for s in range(ROW_TILE)],
                           axis=1)


def _row_copy(src, si, dst, di, sem):
    return pltpu.make_async_copy(
        src.at[pl.ds(pl.multiple_of(si * ROW_TILE, ROW_TILE), ROW_TILE), :],
        dst.at[pl.ds(pl.multiple_of(di * ROW_TILE, ROW_TILE), ROW_TILE), :], sem)


def _wait_rows(src, dst, n_rows, sem):
    for _ in range(0, n_rows, WAIT_ROWS):
        pltpu.make_async_copy(src.at[pl.ds(0, WAIT_ROWS * ROW_TILE), :],
                              dst.at[pl.ds(0, WAIT_ROWS * ROW_TILE), :], sem).wait()


def _dispatch_kernel(pend_ref, npad_ref, dest_ref, hf_ref, xs_ref, zero_ref, sem, zsem):
    tb = dest_ref.shape[1]

    @pl.when(pl.program_id(0) == 0)
    def _():
        zero_ref[...] = jnp.zeros_like(zero_ref)

        def zcopy(e):
            start = pl.multiple_of((pend_ref[e] - MOE_TILE) * ROW_TILE, MOE_TILE * ROW_TILE)
            return pltpu.make_async_copy(
                zero_ref, xs_ref.at[pl.ds(start, MOE_TILE * ROW_TILE), :], zsem)

        def zstart(e, c):
            @pl.when(npad_ref[e] > 0)
            def _():
                zcopy(e).start()
            return c

        def zwait(e, c):
            @pl.when(npad_ref[e] > 0)
            def _():
                zcopy(e).wait()
            return c

        lax.fori_loop(0, N_EXPERTS, zstart, 0)
        lax.fori_loop(0, N_EXPERTS, zwait, 0)

        def tcopy(j):
            start = pl.multiple_of(j * (MOE_TILE * ROW_TILE), MOE_TILE * ROW_TILE)
            return pltpu.make_async_copy(
                zero_ref, xs_ref.at[pl.ds(start, MOE_TILE * ROW_TILE), :], zsem)

        def tstart(j, c):
            tcopy(j).start()
            return c

        def twait(j, c):
            tcopy(j).wait()
            return c

        first_free = pend_ref[N_EXPERTS - 1] // MOE_TILE
        n_blocks = xs_ref.shape[0] // (MOE_TILE * ROW_TILE)
        lax.fori_loop(first_free, n_blocks, tstart, 0)
        lax.fori_loop(first_free, n_blocks, twait, 0)

    def issue(j, c):
        base = pl.multiple_of(j * ISSUE_UNROLL, ISSUE_UNROLL)
        for u in range(ISSUE_UNROLL):
            for kk in range(TOP_K):
                _row_copy(hf_ref, base + u, xs_ref, dest_ref[kk, base + u], sem).start(
                    priority=kk % DMA_THREADS)
        return c

    lax.fori_loop(0, tb // ISSUE_UNROLL, issue, 0)
    _wait_rows(hf_ref, xs_ref, TOP_K * tb, sem)


def _dispatch_call(pad_end, n_pad, dest_t, hf, n_slots):
    T = hf.shape[0] // ROW_TILE
    tb = DISPATCH_BLOCK
    return pl.pallas_call(
        _dispatch_kernel,
        grid_spec=pltpu.PrefetchScalarGridSpec(
            num_scalar_prefetch=2,
            grid=(T // tb,),
            in_specs=[pl.BlockSpec((TOP_K, tb), lambda i, pe, npd: (0, i),
                                   memory_space=pltpu.SMEM),
                      pl.BlockSpec((tb * ROW_TILE, LANES), lambda i, pe, npd: (i, 0))],
            out_specs=pl.BlockSpec(memory_space=pl.ANY),
            scratch_shapes=[pltpu.VMEM((MOE_TILE * ROW_TILE, LANES), F32),
                            pltpu.SemaphoreType.DMA, pltpu.SemaphoreType.DMA]),
        out_shape=jax.ShapeDtypeStruct((n_slots * ROW_TILE, LANES), F32),
        compiler_params=_cparams(("arbitrary",)),
        name="dispatch",
    )(pad_end, n_pad, dest_t, hf)


def _expert_kernel(be_ref, xs_ref, wg_ref, bg_ref, wu_ref, bu_ref, wd_ref, bd_ref, ys_ref,
                   wgb, wub, wdb):
    i = pl.program_id(0)
    prev = be_ref[jnp.maximum(i - 1, 0)]

    @pl.when(jnp.logical_or(i == 0, be_ref[i] != prev))
    def _():
        wgb[...] = wg_ref[0].astype(BF16)
        wub[...] = wu_ref[0].astype(BF16)
        wdb[...] = wd_ref[0].astype(BF16)

    n = MOE_TILE // EXPERT_GROUPS
    views = [pl.ds(gi * n * ROW_TILE, n * ROW_TILE) for gi in range(EXPERT_GROUPS)]
    acts = []
    for view in views:
        x = _load_rows(xs_ref.at[view, :], n).astype(BF16)
        g = jnp.dot(x, wgb[...], preferred_element_type=F32) + bg_ref[0]
        u = jnp.dot(x, wub[...], preferred_element_type=F32) + bu_ref[0]
        g = jnp.minimum(g, SWIGLU_LIMIT)
        u = jnp.clip(u, -SWIGLU_LIMIT, SWIGLU_LIMIT)
        acts.append((g * (1.0 / (1.0 + jnp.exp(-SWIGLU_ALPHA * g))) * (u + 1.0)).astype(BF16))
    for view, a in zip(views, acts):
        y = jnp.dot(a, wdb[...], preferred_element_type=F32) + bd_ref[0]
        _store_rows(ys_ref.at[view, :], y)


def _expert_call(block_e, xs, w):
    n_slots = xs.shape[0] // ROW_TILE
    wspec = lambda r, c: pl.BlockSpec((1, r, c), lambda i, be: (be[i], 0, 0))
    slotb = pl.BlockSpec((MOE_TILE * ROW_TILE, LANES), lambda i, be: (i, 0))
    return pl.pallas_call(
        _expert_kernel,
        grid_spec=pltpu.PrefetchScalarGridSpec(
            num_scalar_prefetch=1,
            grid=(n_slots // MOE_TILE,),
            in_specs=[slotb, wspec(D_MODEL, D_FF), wspec(1, D_FF), wspec(D_MODEL, D_FF),
                      wspec(1, D_FF), wspec(D_FF, D_MODEL), wspec(1, D_MODEL)],
            out_specs=slotb,
            scratch_shapes=[pltpu.VMEM((D_MODEL, D_FF), BF16), pltpu.VMEM((D_MODEL, D_FF), BF16),
                            pltpu.VMEM((D_FF, D_MODEL), BF16)]),
        out_shape=jax.ShapeDtypeStruct((n_slots * ROW_TILE, LANES), F32),
        compiler_params=_cparams(("arbitrary",)),
        name="experts",
    )(block_e, xs, w["w_gate"], w["b_gate"], w["w_up"], w["b_up"], w["w_down"], w["b_down"])


def _combine_kernel(dest_ref, rg_ref, x2_ref, gfin_ref, ys_ref, out_ref, buf, sem, *, final):
    tb = x2_ref.shape[0]

    def issue(j, c):
        base = pl.multiple_of(j * ISSUE_UNROLL, ISSUE_UNROLL)
        for u in range(ISSUE_UNROLL):
            for kk in range(TOP_K):
                _row_copy(ys_ref, dest_ref[kk, base + u], buf.at[kk], base + u, sem).start(
                    priority=kk % DMA_THREADS)
        return c

    lax.fori_loop(0, tb // ISSUE_UNROLL, issue, 0)
    _wait_rows(ys_ref, buf.at[0], TOP_K * tb, sem)

    rg = rg_ref[...]
    y = x2_ref[...]
    for kk in range(TOP_K):
        y = y + _load_rows(buf.at[kk], tb) * rg[:, kk:kk + 1]
    if final:
        y = _rms(y, gfin_ref[...])
    out_ref[...] = y


def _combine_call(dest_t, rg, x2, g_final, ys, *, final):
    T = x2.shape[0]
    tb = COMBINE_BLOCK
    return pl.pallas_call(
        functools.partial(_combine_kernel, final=final),
        grid=(T // tb,),
        in_specs=[pl.BlockSpec((TOP_K, tb), lambda i: (0, i), memory_space=pltpu.SMEM),
                  pl.BlockSpec((tb, LANES), lambda i: (i, 0)),
                  pl.BlockSpec((tb, D_MODEL), lambda i: (i, 0)),
                  pl.BlockSpec((1, D_MODEL), lambda i: (0, 0)),
                  pl.BlockSpec(memory_space=pl.ANY)],
        out_specs=pl.BlockSpec((tb, D_MODEL), lambda i: (i, 0)),
        out_shape=jax.ShapeDtypeStruct((T, D_MODEL), F32),
        scratch_shapes=[pltpu.VMEM((TOP_K, tb * ROW_TILE, LANES), F32),
                        pltpu.SemaphoreType.DMA],
        compiler_params=_cparams(("arbitrary",)),
        name="combine",
    )(dest_t, rg, x2, g_final, ys)


def _rope_tables(S, rot_dim, theta, group, lead):
    half = rot_dim // 2
    inv_freq = jnp.power(theta, -2.0 * jnp.arange(half, dtype=F32) / rot_dim)
    ang = jnp.arange(S, dtype=F32)[:, None] * inv_freq[None, :]
    cos = jnp.cos(ang)
    sin = jnp.sin(ang)
    rest = group - lead - rot_dim
    cg = jnp.concatenate([jnp.ones((S, lead), F32), cos, cos, jnp.ones((S, rest), F32)], axis=1)
    sg = jnp.concatenate([jnp.zeros((S, lead), F32), -sin, sin, jnp.zeros((S, rest), F32)], axis=1)
    reps = LANES // group
    return jnp.tile(cg, (1, reps)), jnp.tile(sg, (1, reps))


def _prep_layer(l, S, g_attn, w_in, g_cq, w_uq, g_ckv, w_uk, w_uv, lam_q1, lam_k1, lam_q2,
                lam_k2, g_subln, w_out, g_ffn, w_router, b_router, w_gate, b_gate, w_up, b_up,
                w_down, b_down):
    w = {}
    wi = w_in[l]
    kpe_cols = jnp.zeros((D_MODEL, LANES), F32).at[:, MLA_NOPE:MLA_NOPE + MLA_ROPE].set(
        wi[:, OFF_KPE:OFF_KPE + MLA_ROPE])
    dv0 = OFF_KPE + MLA_ROPE + 2 * DIFF_QK
    w["w_in"] = jnp.concatenate(
        [wi[:, :OFF_KPE], kpe_cols, wi[:, OFF_KPE + MLA_ROPE:dv0]], axis=1).astype(BF16)
    w["w_dv"] = wi[:, dv0:].astype(BF16)
    dqk = MLA_NOPE + MLA_ROPE
    wq = w_uq[l].reshape(Q_LORA, MLA_HEADS, dqk)
    w["w_uq"] = jnp.pad(wq, ((0, 0), (0, 0), (0, HEAD_PAD - dqk))).reshape(
        Q_LORA, MLA_HEADS * HEAD_PAD).astype(BF16)
    wk = w_uk[l].reshape(KV_LORA, MLA_HEADS, MLA_NOPE)
    w["w_uk"] = jnp.pad(wk, ((0, 0), (0, 0), (0, HEAD_PAD - MLA_NOPE))).reshape(
        KV_LORA, MLA_HEADS * HEAD_PAD).astype(BF16)
    w["w_uv"] = w_uv[l].astype(BF16)
    w["g_attn"] = g_attn[l].reshape(1, D_MODEL)
    w["g_cq"] = g_cq[l].reshape(1, Q_LORA)
    w["g_ckv"] = g_ckv[l].reshape(1, KV_LORA)
    w["cos_m"], w["sin_m"] = _rope_tables(S, MLA_ROPE, MLA_THETA, LANES, MLA_NOPE)
    w["cos_d"], w["sin_d"] = _rope_tables(S, DIFF_ROT, DIFF_THETA, DIFF_HD, 0)
    lam = jnp.stack([lam_q1[l], lam_k1[l], lam_q2[l], lam_k2[l]]).astype(F32)
    w["lam"] = jnp.pad(lam, ((0, 8 - 4), (0, LANES - DIFF_HD)))
    w["g_subln"] = g_subln[l].reshape(2 * DIFF_HD, 1)
    wo = w_out[l].astype(BF16)
    w["w_out_a"] = wo[:MLA_HEADS * MLA_V]
    w["w_out_b"] = wo[MLA_HEADS * MLA_V:]
    w["g_ffn"] = g_ffn[l].reshape(1, D_MODEL)
    wr = jnp.pad(w_router[l], ((0, 0), (0, LANES - N_EXPERTS)))
    w["w_router_hi"] = wr.astype(BF16)
    w["w_router_lo"] = (wr - w["w_router_hi"].astype(F32)).astype(BF16)
    w["b_router"] = jnp.pad(b_router[l].astype(F32), (0, LANES - N_EXPERTS),
                            constant_values=NEG_BIG).reshape(1, LANES)
    w["w_gate"], w["w_up"], w["w_down"] = w_gate[l], w_up[l], w_down[l]
    w["b_gate"] = b_gate[l].reshape(N_EXPERTS, 1, D_FF)
    w["b_up"] = b_up[l].reshape(N_EXPERTS, 1, D_FF)
    w["b_down"] = b_down[l].reshape(N_EXPERTS, 1, D_MODEL)
    w["lambda_init"] = 0.8 - 0.6 * math.exp(-0.3 * l)
    return w


def _layer(x2d, B, S, w, g_final, *, final):
    T = x2d.shape[0]
    q, k, vt, dq, dk, dvt = _proj_call(x2d, S, w)
    o_mla = _attn_call(q, k, vt, B, S, mla=True)
    o_diff = _attn_call(dq, dk, dvt, B, S, mla=False, lam=w["lam"], g=w["g_subln"],
                        lambda_init=w["lambda_init"])
    x2, hf, ri, rg, cnt = _post_call(x2d, o_mla, o_diff, w)

    counts = cnt[0, :N_EXPERTS].astype(I32)
    padded = ((counts + MOE_TILE - 1) // MOE_TILE) * MOE_TILE
    pad_end = jnp.cumsum(padded)
    pad_start = pad_end - padded
    idx = ri[:, :TOP_K]
    dest = pad_start[idx] + ri[:, TOP_K:2 * TOP_K]
    dest_t = dest.T
    n_blocks = (T * TOP_K) // MOE_TILE + N_EXPERTS
    n_slots = n_blocks * MOE_TILE
    blk_start = jnp.arange(n_blocks, dtype=I32) * MOE_TILE
    block_e = jnp.minimum(jnp.sum((pad_end[None, :] <= blk_start[:, None]).astype(I32), axis=1),
                          N_EXPERTS - 1).astype(I32)
    xs = _dispatch_call(pad_end.astype(I32), (padded - counts).astype(I32), dest_t, hf, n_slots)
    ys = _expert_call(block_e, xs, w)
    return _combine_call(dest_t, rg, x2, g_final, ys, final=final)


def kernel(x_prompt, x_sample, g_attn, w_in, g_cq, w_uq, g_ckv, w_uk, w_uv, lam_q1, lam_k1,
           lam_q2, lam_k2, g_subln, w_out, g_ffn, w_router, b_router, w_gate, b_gate, w_up, b_up,
           w_down, b_down, g_final):
    depth = w_in.shape[0]
    gfin = g_final.reshape(1, D_MODEL)

    def trunk(x):
        B, S, D = x.shape
        assert D == D_MODEL and S % TOK_BLOCK == 0
        x2d = x.reshape(B * S, D)
        for l in range(depth):
            w = _prep_layer(l, S, g_attn, w_in, g_cq, w_uq, g_ckv, w_uk, w_uv, lam_q1, lam_k1,
                            lam_q2, lam_k2, g_subln, w_out, g_ffn, w_router, b_router, w_gate,
                            b_gate, w_up, b_up, w_down, b_down)
            x2d = _layer(x2d, B, S, w, gfin, final=(l == depth - 1))
        return x2d.reshape(B, S, D)

    return (trunk(x_prompt), trunk(x_sample))
```

```python
import functools
import math

import jax
import jax.numpy as jnp
from jax import lax
from jax.experimental import pallas as pl
from jax.experimental.pallas import tpu as pltpu

F32 = jnp.float32
BF16 = jnp.bfloat16
I32 = jnp.int32

D_MODEL = 1024
MLA_HEADS = 8
MLA_NOPE = 64
MLA_ROPE = 32
MLA_V = 64
Q_LORA = 384
KV_LORA = 256
MLA_THETA = 10000.0
DIFF_HEADS = 4
DIFF_HD = 64
DIFF_ROT = DIFF_HD // 4
DIFF_THETA = 500000.0
DIFF_QK = DIFF_HEADS * 2 * DIFF_HD
DIFF_VW = DIFF_HEADS * 2 * DIFF_HD
N_EXPERTS = 32
TOP_K = 4
D_FF = 1024
SWIGLU_ALPHA = 1.702
SWIGLU_LIMIT = 7.0
RMS_EPS = 1e-5

LANES = 128
HEAD_PAD = 128
P_IN_PAD = Q_LORA + KV_LORA + LANES + 2 * DIFF_QK
OFF_CKV = Q_LORA
OFF_KPE = Q_LORA + KV_LORA
OFF_DQ = OFF_KPE + LANES
OFF_DK = OFF_DQ + DIFF_QK
LOG2_E = math.log2(math.e)
MLA_SCALE = LOG2_E / math.sqrt(MLA_NOPE + MLA_ROPE)
DIFF_SCALE = LOG2_E / math.sqrt(DIFF_HD)

TOK_BLOCK = 512
Q_TILE = 1024
Q_SUB = 256
KEY_CHUNK = 256
SUB_INTERLEAVE = 1
MOE_TILE = 512
POST_GROUPS = 1
ROW_TILE = D_MODEL // LANES
DISPATCH_BLOCK = 512
COMBINE_BLOCK = 256
ISSUE_UNROLL = 8
DMA_THREADS = 2
WAIT_ROWS = 128
SCORE_LOOKAHEAD = 8
ONES_ROWS = 16
VMEM_LIMIT = 56 * 1024 * 1024


def _rms(x, g):
    ms = jnp.mean(x * x, axis=-1, keepdims=True)
    return x * lax.rsqrt(ms + RMS_EPS) * g


def _cparams(sem):
    return pltpu.CompilerParams(dimension_semantics=sem, vmem_limit_bytes=VMEM_LIMIT)


def _proj_kernel(x_ref, ga_ref, win_ref, wdv_ref, gcq_ref, wuq_ref, gckv_ref, wuk_ref, wuv_ref,
                 cm_ref, sm_ref, cd_ref, sd_ref,
                 q_ref, k_ref, vt_ref, dq_ref, dk_ref, dvt_ref):
    x = x_ref[...]
    tm = x.shape[0]
    h = _rms(x, ga_ref[...]).astype(BF16)
    proj = jnp.dot(h, win_ref[...], preferred_element_type=F32)
    ckv = _rms(proj[:, OFF_CKV:OFF_KPE], gckv_ref[...]).astype(BF16)

    lane = lax.broadcasted_iota(I32, (tm, LANES), 1)
    first_m = (lane & (MLA_ROPE // 2)) == 0
    first_d = (lane & (DIFF_ROT // 2)) == 0
    cm, sm, cd, sd = cm_ref[...], sm_ref[...], cd_ref[...], sd_ref[...]

    def rope_m(v):
        partner = jnp.where(first_m, pltpu.roll(v, LANES - MLA_ROPE // 2, 1),
                            pltpu.roll(v, MLA_ROPE // 2, 1))
        return v * cm + partner * sm

    def rope_d(v):
        partner = jnp.where(first_d, pltpu.roll(v, LANES - DIFF_ROT // 2, 1),
                            pltpu.roll(v, DIFF_ROT // 2, 1))
        return v * cd + partner * sd

    cq = _rms(proj[:, 0:Q_LORA], gcq_ref[...]).astype(BF16)
    q = jnp.dot(cq, wuq_ref[...], preferred_element_type=F32)
    kn = jnp.dot(ckv, wuk_ref[...], preferred_element_type=F32)
    kpe = rope_m(proj[:, OFF_KPE:OFF_DQ])
    for hh in range(MLA_HEADS):
        sl = slice(HEAD_PAD * hh, HEAD_PAD * (hh + 1))
        q_ref[:, sl] = (rope_m(q[:, sl]) * MLA_SCALE).astype(BF16)
        k_ref[:, sl] = (kn[:, sl] + kpe).astype(BF16)

    for i in range(DIFF_QK // LANES):
        sl = slice(LANES * i, LANES * (i + 1))
        dq_ref[:, sl] = (rope_d(proj[:, OFF_DQ + LANES * i:OFF_DQ + LANES * (i + 1)])
                         * DIFF_SCALE).astype(BF16)
        dk_ref[:, sl] = rope_d(proj[:, OFF_DK + LANES * i:OFF_DK + LANES * (i + 1)]).astype(BF16)
    vt_ref[0] = jnp.dot(ckv, wuv_ref[...], preferred_element_type=F32).T.astype(BF16)
    dvt_ref[0] = jnp.dot(h, wdv_ref[...], preferred_element_type=F32).T.astype(BF16)


def _proj_call(x2d, S, w):
    T = x2d.shape[0]
    tm = TOK_BLOCK
    nblk = T // tm
    spb = S // tm
    full = lambda shp: pl.BlockSpec(shp, lambda i: (0,) * len(shp))
    tab = pl.BlockSpec((tm, LANES), lambda i: (i % spb, 0))
    tokb = lambda n: pl.BlockSpec((tm, n), lambda i: (i, 0))
    vtb = pl.BlockSpec((1, DIFF_VW, tm), lambda i: (i, 0, 0))
    return pl.pallas_call(
        _proj_kernel,
        grid=(nblk,),
        in_specs=[tokb(D_MODEL), full((1, D_MODEL)), full((D_MODEL, P_IN_PAD)),
                  full((D_MODEL, DIFF_VW)),
                  full((1, Q_LORA)), full((Q_LORA, MLA_HEADS * HEAD_PAD)),
                  full((1, KV_LORA)), full((KV_LORA, MLA_HEADS * HEAD_PAD)),
                  full((KV_LORA, MLA_HEADS * MLA_V)), tab, tab, tab, tab],
        out_specs=[tokb(MLA_HEADS * HEAD_PAD), tokb(MLA_HEADS * HEAD_PAD), vtb,
                   tokb(DIFF_QK), tokb(DIFF_QK), vtb],
        out_shape=[jax.ShapeDtypeStruct((T, MLA_HEADS * HEAD_PAD), BF16),
                   jax.ShapeDtypeStruct((T, MLA_HEADS * HEAD_PAD), BF16),
                   jax.ShapeDtypeStruct((nblk, MLA_HEADS * MLA_V, tm), BF16),
                   jax.ShapeDtypeStruct((T, DIFF_QK), BF16),
                   jax.ShapeDtypeStruct((T, DIFF_QK), BF16),
                   jax.ShapeDtypeStruct((nblk, DIFF_VW, tm), BF16)],
        compiler_params=_cparams(("arbitrary",)),
        name="proj",
    )(x2d, w["g_attn"], w["w_in"], w["w_dv"], w["g_cq"], w["w_uq"], w["g_ckv"], w["w_uk"],
      w["w_uv"], w["cos_m"], w["sin_m"], w["cos_d"], w["sin_d"])


def _attn_kernel(*refs, mla, lambda_init):
    if mla:
        q_ref, k_ref, vt_ref, o_ref = refs
    else:
        lam_ref, g_ref, q_ref, k_ref, vt_ref, o_ref = refs
    dv = MLA_V if mla else 2 * DIFF_HD
    kc = KEY_CHUNK
    vt_blk = vt_ref.shape[2]
    n_chunks = k_ref.shape[0] // kc
    ones = jnp.ones((ONES_ROWS, kc), BF16)
    if not mla:
        a = lam_ref[...]
        s1 = jnp.sum(a[0:1, :] * a[1:2, :], axis=1, keepdims=True)
        s2 = jnp.sum(a[2:3, :] * a[3:4, :], axis=1, keepdims=True)
        lam = jnp.exp(s1) - jnp.exp(s2) + lambda_init

    def query(t, m):
        rows = slice(Q_SUB * t, Q_SUB * (t + 1))
        if mla:
            return q_ref[rows, HEAD_PAD * m:HEAD_PAD * (m + 1)]
        lane = lax.broadcasted_iota(I32, (Q_SUB, LANES), 1)
        keep = (lane < DIFF_HD) if m == 0 else (lane >= DIFF_HD)
        qf = q_ref[rows, :]
        return jnp.where(keep, qf, jnp.zeros_like(qf))

    def scores(t, c, m):
        if mla:
            kb = k_ref[kc * c:kc * (c + 1), HEAD_PAD * m:HEAD_PAD * (m + 1)]
        else:
            kb = k_ref[kc * c:kc * (c + 1), :]
        return lax.dot_general(kb, qs[t][m], (((1,), (1,)), ((), ())),
                               preferred_element_type=F32)

    def finish(t, acc):
        outs = [acc[m][0:dv] / acc[m][dv:dv + 1] for m in range(2)]
        if mla:
            o = jnp.concatenate(outs, axis=0)
        else:
            o = outs[0] - lam * outs[1]
            ms = jnp.mean(o * o, axis=0, keepdims=True)
            o = o * lax.rsqrt(ms + RMS_EPS) * g_ref[...] * (1.0 - lambda_init)
        o_ref[Q_SUB * t:Q_SUB * (t + 1), :] = o.T.astype(BF16)

    n_sub = q_ref.shape[0] // Q_SUB
    qs = [[query(t, m) for m in range(2)] for t in range(n_sub)]
    order = [(t, c, m) for tp in range(0, n_sub, SUB_INTERLEAVE) for c in range(n_chunks)
             for t in range(tp, tp + SUB_INTERLEAVE) for m in range(2)]
    pending = [scores(*order[i]) for i in range(min(SCORE_LOOKAHEAD, len(order)))]
    m_all = [[jnp.full((1, Q_SUB), -jnp.inf, F32) for _ in range(2)] for _ in range(n_sub)]
    acc_all = [[jnp.zeros((dv + ONES_ROWS, Q_SUB), F32) for _ in range(2)] for _ in range(n_sub)]
    for i, (t, c, m) in enumerate(order):
        m_i, acc = m_all[t], acc_all[t]
        s = pending.pop(0)
        if i + SCORE_LOOKAHEAD < len(order):
            pending.append(scores(*order[i + SCORE_LOOKAHEAD]))
        vrows = slice(MLA_V * m, MLA_V * (m + 1)) if mla else slice(0, dv)
        vcols = slice((kc * c) % vt_blk, (kc * c) % vt_blk + kc)
        vb = jnp.concatenate([vt_ref[(kc * c) // vt_blk, vrows, vcols], ones], axis=0)
        m_new = jnp.maximum(m_i[m], jnp.max(s, axis=0, keepdims=True))
        alpha = jnp.exp2(m_i[m] - m_new)
        p = jnp.exp2(s - m_new).astype(BF16)
        acc[m] = alpha * acc[m] + jnp.dot(vb, p, preferred_element_type=F32)
        m_i[m] = m_new
        if c == n_chunks - 1 and m == 1:
            finish(t, acc)


def _attn_call(q, k, vt, B, S, *, mla, lam=None, g=None, lambda_init=0.0):
    T = q.shape[0]
    tq = min(Q_TILE, S)
    nq = S // tq
    vt_blk = vt.shape[2]
    assert vt_blk % KEY_CHUNK == 0 and S % vt_blk == 0
    qw = 2 * HEAD_PAD if mla else LANES
    in_specs = [pl.BlockSpec((tq, qw), lambda b, j, i: (b * nq + i, j)),
                pl.BlockSpec((S, qw), lambda b, j, i: (b, j)),
                pl.BlockSpec((S // vt_blk, LANES, vt_blk), lambda b, j, i: (b, j, 0))]
    args = [q, k, vt]
    if not mla:
        in_specs = [pl.BlockSpec((8, LANES), lambda b, j, i: (0, 0)),
                    pl.BlockSpec((2 * DIFF_HD, 1), lambda b, j, i: (0, 0))] + in_specs
        args = [lam, g] + args
    return pl.pallas_call(
        functools.partial(_attn_kernel, mla=mla, lambda_init=lambda_init),
        grid=(B, 4, nq),
        in_specs=in_specs,
        out_specs=pl.BlockSpec((tq, LANES), lambda b, j, i: (b * nq + i, j)),
        out_shape=jax.ShapeDtypeStruct((T, 4 * LANES), BF16),
        compiler_params=_cparams(("arbitrary", "arbitrary", "arbitrary")),
        name="attn_mla" if mla else "attn_diff",
    )(*args)


def _post_kernel(x_ref, om_ref, od_ref, woa_ref, wob_ref, gf_ref, wrh_ref, wrl_ref, br_ref,
                 x2_ref, hf_ref, ri_ref, rg_ref, cnt_ref, base_ref):
    step = pl.program_id(0)

    @pl.when(step == 0)
    def _():
        base_ref[...] = jnp.zeros_like(base_ref)

    n = x_ref.shape[0] // POST_GROUPS
    logits = []
    for gi in range(POST_GROUPS):
        rows = slice(gi * n, (gi + 1) * n)
        x2 = (x_ref[rows, :]
              + jnp.dot(om_ref[rows, :], woa_ref[...], preferred_element_type=F32)
              + jnp.dot(od_ref[rows, :], wob_ref[...], preferred_element_type=F32))
        x2_ref[rows, :] = x2
        hf = _rms(x2, gf_ref[...])
        _store_rows(hf_ref.at[pl.ds(gi * n * ROW_TILE, n * ROW_TILE), :], hf)
        h_hi = hf.astype(BF16)
        h_lo = (hf - h_hi.astype(F32)).astype(BF16)
        logits.append(jnp.dot(h_hi, wrh_ref[...], preferred_element_type=F32)
                      + jnp.dot(h_lo, wrh_ref[...], preferred_element_type=F32)
                      + jnp.dot(h_hi, wrl_ref[...], preferred_element_type=F32)
                      + br_ref[...])

    lane = lax.broadcasted_iota(I32, (n, LANES), 1).astype(F32)
    row = lax.broadcasted_iota(I32, (n, n), 0)
    col = lax.broadcasted_iota(I32, (n, n), 1)
    lower = jnp.where(row > col, 1.0, 0.0).astype(BF16)
    base = base_ref[0:1, :]
    for gi in range(POST_GROUPS):
        rows = slice(gi * n, (gi + 1) * n)
        work = logits[gi]
        sels, vals, idxs = [], [], []
        for _ in range(TOP_K):
            mx = jnp.max(work, axis=1, keepdims=True)
            idx = jnp.min(jnp.where(work == mx, lane, float(LANES)), axis=1, keepdims=True)
            sel = lane == idx
            work = jnp.where(sel, -jnp.inf, work)
            sels.append(sel)
            vals.append(mx)
            idxs.append(idx)
        es = [jnp.exp(v - vals[0]) for v in vals]
        den = es[0] + es[1] + es[2] + es[3]
        gates = [e / den for e in es]

        multi = jnp.zeros((n, LANES), F32)
        for sel in sels:
            multi = multi + jnp.where(sel, 1.0, 0.0)
        prefix = jnp.dot(lower, multi.astype(BF16), preferred_element_type=F32) + base
        ri = jnp.zeros((n, LANES), F32)
        rg = jnp.zeros((n, LANES), F32)
        for kk in range(TOP_K):
            rank = jnp.sum(jnp.where(sels[kk], prefix, 0.0), axis=1, keepdims=True)
            ri = jnp.where(lane == float(kk), idxs[kk], ri)
            ri = jnp.where(lane == float(TOP_K + kk), rank, ri)
            rg = jnp.where(lane == float(kk), gates[kk], rg)
        ri_ref[rows, :] = ri.astype(I32)
        rg_ref[rows, :] = rg
        base = base + jnp.sum(multi, axis=0, keepdims=True)
    base_ref[0:1, :] = base
    cnt_ref[...] = jnp.broadcast_to(base, cnt_ref.shape)


def _post_call(x2d, o_mla, o_diff, w):
    T = x2d.shape[0]
    tm = TOK_BLOCK
    full = lambda shp: pl.BlockSpec(shp, lambda i: (0,) * len(shp))
    tokb = lambda n: pl.BlockSpec((tm, n), lambda i: (i, 0))
    return pl.pallas_call(
        _post_kernel,
        grid=(T // tm,),
        in_specs=[tokb(D_MODEL), tokb(4 * LANES), tokb(4 * LANES),
                  full((4 * LANES, D_MODEL)), full((4 * LANES, D_MODEL)), full((1, D_MODEL)),
                  full((D_MODEL, LANES)), full((D_MODEL, LANES)), full((1, LANES))],
        out_specs=[tokb(D_MODEL), pl.BlockSpec((tm * ROW_TILE, LANES), lambda i: (i, 0)),
                   tokb(LANES), tokb(LANES), full((8, LANES))],
        out_shape=[jax.ShapeDtypeStruct((T, D_MODEL), F32),
                   jax.ShapeDtypeStruct((T * ROW_TILE, LANES), F32),
                   jax.ShapeDtypeStruct((T, LANES), I32),
                   jax.ShapeDtypeStruct((T, LANES), F32),
                   jax.ShapeDtypeStruct((8, LANES), F32)],
        scratch_shapes=[pltpu.VMEM((8, LANES), F32)],
        compiler_params=_cparams(("arbitrary",)),
        name="post",
    )(x2d, o_mla, o_diff, w["w_out_a"], w["w_out_b"], w["g_ffn"],
      w["w_router_hi"], w["w_router_lo"], w["b_router"])


def _store_rows(ref, val):
    n = val.shape[0]
    for s in range(ROW_TILE):
        ref[pl.ds(s, n, stride=ROW_TILE), :] = val[:, LANES * s:LANES * (s + 1)]


def _load_rows(ref, n):
    return jnp.concatenate([ref[pl.ds(s, n, stride=ROW_TILE), :] for s in range(ROW_TILE)],
                           axis=1)


def _row_copy(src, si, dst, di, sem):
    return pltpu.make_async_copy(
        src.at[pl.ds(pl.multiple_of(si * ROW_TILE, ROW_TILE), ROW_TILE), :],
        dst.at[pl.ds(pl.multiple_of(di * ROW_TILE, ROW_TILE), ROW_TILE), :], sem)


def _wait_rows(src, dst, n_rows, sem):
    for _ in range(0, n_rows, WAIT_ROWS):
        pltpu.make_async_copy(src.at[pl.ds(0, WAIT_ROWS * ROW_TILE), :],
                              dst.at[pl.ds(0, WAIT_ROWS * ROW_TILE), :], sem).wait()


def _dispatch_kernel(pend_ref, npad_ref, dest_ref, hf_ref, xs_ref, zero_ref, sem, zsem):
    tb = dest_ref.shape[1]

    @pl.when(pl.program_id(0) == 0)
    def _():
        zero_ref[...] = jnp.zeros_like(zero_ref)

        def zcopy(e):
            start = pl.multiple_of((pend_ref[e] - MOE_TILE) * ROW_TILE, MOE_TILE * ROW_TILE)
            return pltpu.make_async_copy(
                zero_ref, xs_ref.at[pl.ds(start, MOE_TILE * ROW_TILE), :], zsem)

        def zstart(e, c):
            @pl.when(npad_ref[e] > 0)
            def _():
                zcopy(e).start()
            return c

        def zwait(e, c):
            @pl.when(npad_ref[e] > 0)
            def _():
                zcopy(e).wait()
            return c

        lax.fori_loop(0, N_EXPERTS, zstart, 0)
        lax.fori_loop(0, N_EXPERTS, zwait, 0)

        def tcopy(j):
            start = pl.multiple_of(j * (MOE_TILE * ROW_TILE), MOE_TILE * ROW_TILE)
            return pltpu.make_async_copy(
                zero_ref, xs_ref.at[pl.ds(start, MOE_TILE * ROW_TILE), :], zsem)

        def tstart(j, c):
            tcopy(j).start()
            return c

        def twait(j, c):
            tcopy(j).wait()
            return c

        first_free = pend_ref[N_EXPERTS - 1] // MOE_TILE
        n_blocks = xs_ref.shape[0] // (MOE_TILE * ROW_TILE)
        lax.fori_loop(first_free, n_blocks, tstart, 0)
        lax.fori_loop(first_free, n_blocks, twait, 0)

    def issue(j, c):
        base = pl.multiple_of(j * ISSUE_UNROLL, ISSUE_UNROLL)
        for u in range(ISSUE_UNROLL):
            for kk in range(TOP_K):
                _row_copy(hf_ref, base + u, xs_ref, dest_ref[kk, base + u], sem).start(
                    priority=kk % DMA_THREADS)
        return c

    lax.fori_loop(0, tb // ISSUE_UNROLL, issue, 0)
    _wait_rows(hf_ref, xs_ref, TOP_K * tb, sem)


def _dispatch_call(pad_end, n_pad, dest_t, hf, n_slots):
    T = hf.shape[0] // ROW_TILE
    tb = DISPATCH_BLOCK
    return pl.pallas_call(
        _dispatch_kernel,
        grid_spec=pltpu.PrefetchScalarGridSpec(
            num_scalar_prefetch=2,
            grid=(T // tb,),
            in_specs=[pl.BlockSpec((TOP_K, tb), lambda i, pe, npd: (0, i),
                                   memory_space=pltpu.SMEM),
                      pl.BlockSpec((tb * ROW_TILE, LANES), lambda i, pe, npd: (i, 0))],
            out_specs=pl.BlockSpec(memory_space=pl.ANY),
            scratch_shapes=[pltpu.VMEM((MOE_TILE * ROW_TILE, LANES), F32),
                            pltpu.SemaphoreType.DMA, pltpu.SemaphoreType.DMA]),
        out_shape=jax.ShapeDtypeStruct((n_slots * ROW_TILE, LANES), F32),
        compiler_params=_cparams(("arbitrary",)),
        name="dispatch",
    )(pad_end, n_pad, dest_t, hf)


def _expert_kernel(be_ref, nlive_ref, xs_ref, wg_ref, bg_ref, wu_ref, bu_ref, wd_ref, bd_ref,
                   ys_ref, wgb, wub, wdb):
    i = pl.program_id(0)
    prev = be_ref[jnp.maximum(i - 1, 0)]

    @pl.when(jnp.logical_or(i == 0, be_ref[i] != prev))
    def _():
        wgb[...] = wg_ref[0].astype(BF16)
        wub[...] = wu_ref[0].astype(BF16)
        wdb[...] = wd_ref[0].astype(BF16)

    live = i < nlive_ref[0]

    @pl.when(live)
    def _():
        x = _load_rows(xs_ref, MOE_TILE).astype(BF16)
        g = jnp.dot(x, wgb[...], preferred_element_type=F32) + bg_ref[0]
        u = jnp.dot(x, wub[...], preferred_element_type=F32) + bu_ref[0]
        g = jnp.minimum(g, SWIGLU_LIMIT)
        u = jnp.clip(u, -SWIGLU_LIMIT, SWIGLU_LIMIT)
        a = g * (1.0 / (1.0 + jnp.exp(-SWIGLU_ALPHA * g))) * (u + 1.0)
        y = jnp.dot(a.astype(BF16), wdb[...], preferred_element_type=F32) + bd_ref[0]
        _store_rows(ys_ref, y)

    @pl.when(jnp.logical_not(live))
    def _():
        ys_ref[...] = jnp.zeros_like(ys_ref)


def _expert_call(block_e, n_live, xs, w):
    n_slots = xs.shape[0] // ROW_TILE
    wspec = lambda r, c: pl.BlockSpec((1, r, c), lambda i, be, nl: (be[i], 0, 0))
    slotb = pl.BlockSpec((MOE_TILE * ROW_TILE, LANES), lambda i, be, nl: (i, 0))
    return pl.pallas_call(
        _expert_kernel,
        grid_spec=pltpu.PrefetchScalarGridSpec(
            num_scalar_prefetch=2,
            grid=(n_slots // MOE_TILE,),
            in_specs=[slotb, wspec(D_MODEL, D_FF), wspec(1, D_FF), wspec(D_MODEL, D_FF),
                      wspec(1, D_FF), wspec(D_FF, D_MODEL), wspec(1, D_MODEL)],
            out_specs=slotb,
            scratch_shapes=[pltpu.VMEM((D_MODEL, D_FF), BF16), pltpu.VMEM((D_MODEL, D_FF), BF16),
                            pltpu.VMEM((D_FF, D_MODEL), BF16)]),
        out_shape=jax.ShapeDtypeStruct((n_slots * ROW_TILE, LANES), F32),
        compiler_params=_cparams(("arbitrary",)),
        name="experts",
    )(block_e, n_live, xs, w["w_gate"], w["b_gate"], w["w_up"], w["b_up"], w["w_down"],
      w["b_down"])


def _combine_kernel(dest_ref, rg_ref, x2_ref, gfin_ref, ys_ref, out_ref, buf, sem, *, final):
    tb = x2_ref.shape[0]

    def issue(j, c):
        base = pl.multiple_of(j * ISSUE_UNROLL, ISSUE_UNROLL)
        for u in range(ISSUE_UNROLL):
            for kk in range(TOP_K):
                _row_copy(ys_ref, dest_ref[kk, base + u], buf.at[kk], base + u, sem).start(
                    priority=kk % DMA_THREADS)
        return c

    lax.fori_loop(0, tb // ISSUE_UNROLL, issue, 0)
    _wait_rows(ys_ref, buf.at[0], TOP_K * tb, sem)

    rg = rg_ref[...]
    y = x2_ref[...]
    for kk in range(TOP_K):
        y = y + _load_rows(buf.at[kk], tb) * rg[:, kk:kk + 1]
    if final:
        y = _rms(y, gfin_ref[...])
    out_ref[...] = y


def _combine_call(dest_t, rg, x2, g_final, ys, *, final):
    T = x2.shape[0]
    tb = COMBINE_BLOCK
    return pl.pallas_call(
        functools.partial(_combine_kernel, final=final),
        grid=(T // tb,),
        in_specs=[pl.BlockSpec((TOP_K, tb), lambda i: (0, i), memory_space=pltpu.SMEM),
                  pl.BlockSpec((tb, LANES), lambda i: (i, 0)),
                  pl.BlockSpec((tb, D_MODEL), lambda i: (i, 0)),
                  pl.BlockSpec((1, D_MODEL), lambda i: (0, 0)),
                  pl.BlockSpec(memory_space=pl.ANY)],
        out_specs=pl.BlockSpec((tb, D_MODEL), lambda i: (i, 0)),
        out_shape=jax.ShapeDtypeStruct((T, D_MODEL), F32),
        scratch_shapes=[pltpu.VMEM((TOP_K, tb * ROW_TILE, LANES), F32),
                        pltpu.SemaphoreType.DMA],
        compiler_params=_cparams(("arbitrary",)),
        name="combine",
    )(dest_t, rg, x2, g_final, ys)


def _rope_tables(S, rot_dim, theta, group, lead):
    half = rot_dim // 2
    inv_freq = jnp.power(theta, -2.0 * jnp.arange(half, dtype=F32) / rot_dim)
    ang = jnp.arange(S, dtype=F32)[:, None] * inv_freq[None, :]
    cos = jnp.cos(ang)
    sin = jnp.sin(ang)
    rest = group - lead - rot_dim
    cg = jnp.concatenate([jnp.ones((S, lead), F32), cos, cos, jnp.ones((S, rest), F32)], axis=1)
    sg = jnp.concatenate([jnp.zeros((S, lead), F32), -sin, sin, jnp.zeros((S, rest), F32)], axis=1)
    reps = LANES // group
    return jnp.tile(cg, (1, reps)), jnp.tile(sg, (1, reps))


def _prep_layer(l, S, g_attn, w_in, g_cq, w_uq, g_ckv, w_uk, w_uv, lam_q1, lam_k1, lam_q2,
                lam_k2, g_subln, w_out, g_ffn, w_router, b_router, w_gate, b_gate, w_up, b_up,
                w_down, b_down):
    w = {}
    wi = w_in[l]
    kpe_cols = jnp.zeros((D_MODEL, LANES), F32).at[:, MLA_NOPE:MLA_NOPE + MLA_ROPE].set(
        wi[:, OFF_KPE:OFF_KPE + MLA_ROPE])
    dv0 = OFF_KPE + MLA_ROPE + 2 * DIFF_QK
    w["w_in"] = jnp.concatenate(
        [wi[:, :OFF_KPE], kpe_cols, wi[:, OFF_KPE + MLA_ROPE:dv0]], axis=1).astype(BF16)
    w["w_dv"] = wi[:, dv0:].astype(BF16)
    dqk = MLA_NOPE + MLA_ROPE
    wq = w_uq[l].reshape(Q_LORA, MLA_HEADS, dqk)
    w["w_uq"] = jnp.pad(wq, ((0, 0), (0, 0), (0, HEAD_PAD - dqk))).reshape(
        Q_LORA, MLA_HEADS * HEAD_PAD).astype(BF16)
    wk = w_uk[l].reshape(KV_LORA, MLA_HEADS, MLA_NOPE)
    w["w_uk"] = jnp.pad(wk, ((0, 0), (0, 0), (0, HEAD_PAD - MLA_NOPE))).reshape(
        KV_LORA, MLA_HEADS * HEAD_PAD).astype(BF16)
    w["w_uv"] = w_uv[l].astype(BF16)
    w["g_attn"] = g_attn[l].reshape(1, D_MODEL)
    w["g_cq"] = g_cq[l].reshape(1, Q_LORA)
    w["g_ckv"] = g_ckv[l].reshape(1, KV_LORA)
    w["cos_m"], w["sin_m"] = _rope_tables(S, MLA_ROPE, MLA_THETA, LANES, MLA_NOPE)
    w["cos_d"], w["sin_d"] = _rope_tables(S, DIFF_ROT, DIFF_THETA, DIFF_HD, 0)
    lam = jnp.stack([lam_q1[l], lam_k1[l], lam_q2[l], lam_k2[l]]).astype(F32)
    w["lam"] = jnp.pad(lam, ((0, 8 - 4), (0, LANES - DIFF_HD)))
    w["g_subln"] = g_subln[l].reshape(2 * DIFF_HD, 1)
    wo = w_out[l].astype(BF16)
    w["w_out_a"] = wo[:MLA_HEADS * MLA_V]
    w["w_out_b"] = wo[MLA_HEADS * MLA_V:]
    w["g_ffn"] = g_ffn[l].reshape(1, D_MODEL)
    wr = jnp.pad(w_router[l], ((0, 0), (0, LANES - N_EXPERTS)))
    w["w_router_hi"] = wr.astype(BF16)
    w["w_router_lo"] = (wr - w["w_router_hi"].astype(F32)).astype(BF16)
    w["b_router"] = jnp.pad(b_router[l].astype(F32), (0, LANES - N_EXPERTS),
                            constant_values=-jnp.inf).reshape(1, LANES)
    w["w_gate"], w["w_up"], w["w_down"] = w_gate[l], w_up[l], w_down[l]
    w["b_gate"] = b_gate[l].reshape(N_EXPERTS, 1, D_FF)
    w["b_up"] = b_up[l].reshape(N_EXPERTS, 1, D_FF)
    w["b_down"] = b_down[l].reshape(N_EXPERTS, 1, D_MODEL)
    w["lambda_init"] = 0.8 - 0.6 * math.exp(-0.3 * l)
    return w


def _layer(x2d, B, S, w, g_final, *, final):
    T = x2d.shape[0]
    q, k, vt, dq, dk, dvt = _proj_call(x2d, S, w)
    o_mla = _attn_call(q, k, vt, B, S, mla=True)
    o_diff = _attn_call(dq, dk, dvt, B, S, mla=False, lam=w["lam"], g=w["g_subln"],
                        lambda_init=w["lambda_init"])
    x2, hf, ri, rg, cnt = _post_call(x2d, o_mla, o_diff, w)

    counts = cnt[0, :N_EXPERTS].astype(I32)
    padded = ((counts + MOE_TILE - 1) // MOE_TILE) * MOE_TILE
    pad_end = jnp.cumsum(padded)
    pad_start = pad_end - padded
    idx = ri[:, :TOP_K]
    dest = pad_start[idx] + ri[:, TOP_K:2 * TOP_K]
    dest_t = dest.T
    n_blocks = (T * TOP_K) // MOE_TILE + N_EXPERTS
    n_slots = n_blocks * MOE_TILE
    blk_start = jnp.arange(n_blocks, dtype=I32) * MOE_TILE
    block_e = jnp.minimum(jnp.sum((pad_end[None, :] <= blk_start[:, None]).astype(I32), axis=1),
                          N_EXPERTS - 1).astype(I32)
    xs = _dispatch_call(pad_end.astype(I32), (padded - counts).astype(I32), dest_t, hf, n_slots)
    n_live = (pad_end[N_EXPERTS - 1:] // MOE_TILE).astype(I32)
    ys = _expert_call(block_e, n_live, xs, w)
    return _combine_call(dest_t, rg, x2, g_final, ys, final=final)


def kernel(x_prompt, x_sample, g_attn, w_in, g_cq, w_uq, g_ckv, w_uk, w_uv, lam_q1, lam_k1,
           lam_q2, lam_k2, g_subln, w_out, g_ffn, w_router, b_router, w_gate, b_gate, w_up, b_up,
           w_down, b_down, g_final):
    depth = w_in.shape[0]
    gfin = g_final.reshape(1, D_MODEL)

    def trunk(x):
        B, S, D = x.shape
        assert D == D_MODEL and S % TOK_BLOCK == 0
        x2d = x.reshape(B * S, D)
        for l in range(depth):
            w = _prep_layer(l, S, g_attn, w_in, g_cq, w_uq, g_ckv, w_uk, w_uv, lam_q1, lam_k1,
                            lam_q2, lam_k2, g_subln, w_out, g_ffn, w_router, b_router, w_gate,
                            b_gate, w_up, b_up, w_down, b_down)
            x2d = _layer(x2d, B, S, w, gfin, final=(l == depth - 1))
        return x2d.reshape(B, S, D)

    return (trunk(x_prompt), trunk(x_sample))
```

```python
import functools
import math

import jax
import jax.numpy as jnp
from jax import lax
from jax.experimental import pallas as pl
from jax.experimental.pallas import tpu as pltpu

F32 = jnp.float32
BF16 = jnp.bfloat16
I32 = jnp.int32

D_MODEL = 1024
MLA_HEADS = 8
MLA_NOPE = 64
MLA_ROPE = 32
MLA_V = 64
Q_LORA = 384
KV_LORA = 256
MLA_THETA = 10000.0
DIFF_HEADS = 4
DIFF_HD = 64
DIFF_ROT = DIFF_HD // 4
DIFF_THETA = 500000.0
DIFF_QK = DIFF_HEADS * 2 * DIFF_HD
DIFF_VW = DIFF_HEADS * 2 * DIFF_HD
N_EXPERTS = 32
TOP_K = 4
D_FF = 1024
SWIGLU_ALPHA = 1.702
SWIGLU_LIMIT = 7.0
RMS_EPS = 1e-5

LANES = 128
HEAD_PAD = 128
P_IN_PAD = Q_LORA + KV_LORA + LANES + 2 * DIFF_QK
OFF_CKV = Q_LORA
OFF_KPE = Q_LORA + KV_LORA
OFF_DQ = OFF_KPE + LANES
OFF_DK = OFF_DQ + DIFF_QK
LOG2_E = math.log2(math.e)
MLA_SCALE = LOG2_E / math.sqrt(MLA_NOPE + MLA_ROPE)
DIFF_SCALE = LOG2_E / math.sqrt(DIFF_HD)

TOK_BLOCK = 512
Q_TILE = 1024
Q_SUB = 256
KEY_CHUNK = 256
SUB_INTERLEAVE = 1
MOE_TILE = 512
POST_GROUPS = 1
ROW_TILE = D_MODEL // LANES
DISPATCH_BLOCK = 1024
COMBINE_BLOCK = 256
ISSUE_UNROLL = 8
DMA_THREADS = 2
WAIT_ROWS = 128
SCORE_LOOKAHEAD = 8
ONES_ROWS = 16
VMEM_LIMIT = 56 * 1024 * 1024


def _rms(x, g):
    ms = jnp.mean(x * x, axis=-1, keepdims=True)
    return x * lax.rsqrt(ms + RMS_EPS) * g


def _cparams(sem):
    return pltpu.CompilerParams(dimension_semantics=sem, vmem_limit_bytes=VMEM_LIMIT)


def _proj_kernel(x_ref, ga_ref, win_ref, wdv_ref, gcq_ref, wuq_ref, gckv_ref, wuk_ref, wuv_ref,
                 cm_ref, sm_ref, cd_ref, sd_ref,
                 q_ref, k_ref, vt_ref, dq_ref, dk_ref, dvt_ref):
    x = x_ref[...]
    tm = x.shape[0]
    h = _rms(x, ga_ref[...]).astype(BF16)
    proj = jnp.dot(h, win_ref[...], preferred_element_type=F32)
    ckv = _rms(proj[:, OFF_CKV:OFF_KPE], gckv_ref[...]).astype(BF16)

    lane = lax.broadcasted_iota(I32, (tm, LANES), 1)
    first_m = (lane & (MLA_ROPE // 2)) == 0
    first_d = (lane & (DIFF_ROT // 2)) == 0
    cm, sm, cd, sd = cm_ref[...], sm_ref[...], cd_ref[...], sd_ref[...]

    def rope_m(v):
        partner = jnp.where(first_m, pltpu.roll(v, LANES - MLA_ROPE // 2, 1),
                            pltpu.roll(v, MLA_ROPE // 2, 1))
        return v * cm + partner * sm

    def rope_d(v):
        partner = jnp.where(first_d, pltpu.roll(v, LANES - DIFF_ROT // 2, 1),
                            pltpu.roll(v, DIFF_ROT // 2, 1))
        return v * cd + partner * sd

    cq = _rms(proj[:, 0:Q_LORA], gcq_ref[...]).astype(BF16)
    q = jnp.dot(cq, wuq_ref[...], preferred_element_type=F32)
    kn = jnp.dot(ckv, wuk_ref[...], preferred_element_type=F32)
    kpe = rope_m(proj[:, OFF_KPE:OFF_DQ])
    for hh in range(MLA_HEADS):
        sl = slice(HEAD_PAD * hh, HEAD_PAD * (hh + 1))
        q_ref[:, sl] = (rope_m(q[:, sl]) * MLA_SCALE).astype(BF16)
        k_ref[:, sl] = (kn[:, sl] + kpe).astype(BF16)

    for i in range(DIFF_QK // LANES):
        sl = slice(LANES * i, LANES * (i + 1))
        dq_ref[:, sl] = (rope_d(proj[:, OFF_DQ + LANES * i:OFF_DQ + LANES * (i + 1)])
                         * DIFF_SCALE).astype(BF16)
        dk_ref[:, sl] = rope_d(proj[:, OFF_DK + LANES * i:OFF_DK + LANES * (i + 1)]).astype(BF16)
    vt_ref[0] = jnp.dot(ckv, wuv_ref[...], preferred_element_type=F32).T.astype(BF16)
    dvt_ref[0] = jnp.dot(h, wdv_ref[...], preferred_element_type=F32).T.astype(BF16)


def _proj_call(x2d, S, w):
    T = x2d.shape[0]
    tm = TOK_BLOCK
    nblk = T // tm
    spb = S // tm
    full = lambda shp: pl.BlockSpec(shp, lambda i: (0,) * len(shp))
    tab = pl.BlockSpec((tm, LANES), lambda i: (i % spb, 0))
    tokb = lambda n: pl.BlockSpec((tm, n), lambda i: (i, 0))
    vtb = pl.BlockSpec((1, DIFF_VW, tm), lambda i: (i, 0, 0))
    return pl.pallas_call(
        _proj_kernel,
        grid=(nblk,),
        in_specs=[tokb(D_MODEL), full((1, D_MODEL)), full((D_MODEL, P_IN_PAD)),
                  full((D_MODEL, DIFF_VW)),
                  full((1, Q_LORA)), full((Q_LORA, MLA_HEADS * HEAD_PAD)),
                  full((1, KV_LORA)), full((KV_LORA, MLA_HEADS * HEAD_PAD)),
                  full((KV_LORA, MLA_HEADS * MLA_V)), tab, tab, tab, tab],
        out_specs=[tokb(MLA_HEADS * HEAD_PAD), tokb(MLA_HEADS * HEAD_PAD), vtb,
                   tokb(DIFF_QK), tokb(DIFF_QK), vtb],
        out_shape=[jax.ShapeDtypeStruct((T, MLA_HEADS * HEAD_PAD), BF16),
                   jax.ShapeDtypeStruct((T, MLA_HEADS * HEAD_PAD), BF16),
                   jax.ShapeDtypeStruct((nblk, MLA_HEADS * MLA_V, tm), BF16),
                   jax.ShapeDtypeStruct((T, DIFF_QK), BF16),
                   jax.ShapeDtypeStruct((T, DIFF_QK), BF16),
                   jax.ShapeDtypeStruct((nblk, DIFF_VW, tm), BF16)],
        compiler_params=_cparams(("arbitrary",)),
        name="proj",
    )(x2d, w["g_attn"], w["w_in"], w["w_dv"], w["g_cq"], w["w_uq"], w["g_ckv"], w["w_uk"],
      w["w_uv"], w["cos_m"], w["sin_m"], w["cos_d"], w["sin_d"])


def _attn_kernel(*refs, mla, lambda_init):
    if mla:
        q_ref, k_ref, vt_ref, o_ref = refs
    else:
        lam_ref, g_ref, q_ref, k_ref, vt_ref, o_ref = refs
    dv = MLA_V if mla else 2 * DIFF_HD
    kc = KEY_CHUNK
    vt_blk = vt_ref.shape[2]
    n_chunks = k_ref.shape[0] // kc
    ones = jnp.ones((ONES_ROWS, kc), BF16)
    if not mla:
        a = lam_ref[...]
        s1 = jnp.sum(a[0:1, :] * a[1:2, :], axis=1, keepdims=True)
        s2 = jnp.sum(a[2:3, :] * a[3:4, :], axis=1, keepdims=True)
        lam = jnp.exp(s1) - jnp.exp(s2) + lambda_init

    def query(t, m):
        rows = slice(Q_SUB * t, Q_SUB * (t + 1))
        if mla:
            return q_ref[rows, HEAD_PAD * m:HEAD_PAD * (m + 1)]
        lane = lax.broadcasted_iota(I32, (Q_SUB, LANES), 1)
        keep = (lane < DIFF_HD) if m == 0 else (lane >= DIFF_HD)
        qf = q_ref[rows, :]
        return jnp.where(keep, qf, jnp.zeros_like(qf))

    def scores(t, c, m):
        if mla:
            kb = k_ref[kc * c:kc * (c + 1), HEAD_PAD * m:HEAD_PAD * (m + 1)]
        else:
            kb = k_ref[kc * c:kc * (c + 1), :]
        return lax.dot_general(kb, qs[t][m], (((1,), (1,)), ((), ())),
                               preferred_element_type=F32)

    def finish(t, acc):
        outs = [acc[m][0:dv] / acc[m][dv:dv + 1] for m in range(2)]
        if mla:
            o = jnp.concatenate(outs, axis=0)
        else:
            o = outs[0] - lam * outs[1]
            ms = jnp.mean(o * o, axis=0, keepdims=True)
            o = o * lax.rsqrt(ms + RMS_EPS) * g_ref[...] * (1.0 - lambda_init)
        o_ref[Q_SUB * t:Q_SUB * (t + 1), :] = o.T.astype(BF16)

    n_sub = q_ref.shape[0] // Q_SUB
    qs = [[query(t, m) for m in range(2)] for t in range(n_sub)]
    order = [(t, c, m) for tp in range(0, n_sub, SUB_INTERLEAVE) for c in range(n_chunks)
             for t in range(tp, tp + SUB_INTERLEAVE) for m in range(2)]
    pending = [scores(*order[i]) for i in range(min(SCORE_LOOKAHEAD, len(order)))]
    m_all = [[jnp.full((1, Q_SUB), -jnp.inf, F32) for _ in range(2)] for _ in range(n_sub)]
    acc_all = [[jnp.zeros((dv + ONES_ROWS, Q_SUB), F32) for _ in range(2)] for _ in range(n_sub)]
    for i, (t, c, m) in enumerate(order):
        m_i, acc = m_all[t], acc_all[t]
        s = pending.pop(0)
        if i + SCORE_LOOKAHEAD < len(order):
            pending.append(scores(*order[i + SCORE_LOOKAHEAD]))
        vrows = slice(MLA_V * m, MLA_V * (m + 1)) if mla else slice(0, dv)
        vcols = slice((kc * c) % vt_blk, (kc * c) % vt_blk + kc)
        vb = jnp.concatenate([vt_ref[(kc * c) // vt_blk, vrows, vcols], ones], axis=0)
        m_new = jnp.maximum(m_i[m], jnp.max(s, axis=0, keepdims=True))
        alpha = jnp.exp2(m_i[m] - m_new)
        p = jnp.exp2(s - m_new).astype(BF16)
        acc[m] = alpha * acc[m] + jnp.dot(vb, p, preferred_element_type=F32)
        m_i[m] = m_new
        if c == n_chunks - 1 and m == 1:
            finish(t, acc)


def _attn_call(q, k, vt, B, S, *, mla, lam=None, g=None, lambda_init=0.0):
    T = q.shape[0]
    tq = min(Q_TILE, S)
    nq = S // tq
    vt_blk = vt.shape[2]
    assert vt_blk % KEY_CHUNK == 0 and S % vt_blk == 0
    qw = 2 * HEAD_PAD if mla else LANES
    in_specs = [pl.BlockSpec((tq, qw), lambda b, j, i: (b * nq + i, j)),
                pl.BlockSpec((S, qw), lambda b, j, i: (b, j)),
                pl.BlockSpec((S // vt_blk, LANES, vt_blk), lambda b, j, i: (b, j, 0))]
    args = [q, k, vt]
    if not mla:
        in_specs = [pl.BlockSpec((8, LANES), lambda b, j, i: (0, 0)),
                    pl.BlockSpec((2 * DIFF_HD, 1), lambda b, j, i: (0, 0))] + in_specs
        args = [lam, g] + args
    return pl.pallas_call(
        functools.partial(_attn_kernel, mla=mla, lambda_init=lambda_init),
        grid=(B, 4, nq),
        in_specs=in_specs,
        out_specs=pl.BlockSpec((tq, LANES), lambda b, j, i: (b * nq + i, j)),
        out_shape=jax.ShapeDtypeStruct((T, 4 * LANES), BF16),
        compiler_params=_cparams(("arbitrary", "arbitrary", "arbitrary")),
        name="attn_mla" if mla else "attn_diff",
    )(*args)


def _post_kernel(x_ref, om_ref, od_ref, woa_ref, wob_ref, gf_ref, wrh_ref, wrl_ref, br_ref,
                 x2_ref, hf_ref, ri_ref, rg_ref, cnt_ref, base_ref):
    step = pl.program_id(0)

    @pl.when(step == 0)
    def _():
        base_ref[...] = jnp.zeros_like(base_ref)

    n = x_ref.shape[0] // POST_GROUPS
    logits = []
    for gi in range(POST_GROUPS):
        rows = slice(gi * n, (gi + 1) * n)
        x2 = (x_ref[rows, :]
              + jnp.dot(om_ref[rows, :], woa_ref[...], preferred_element_type=F32)
              + jnp.dot(od_ref[rows, :], wob_ref[...], preferred_element_type=F32))
        x2_ref[rows, :] = x2
        hf = _rms(x2, gf_ref[...])
        _store_rows(hf_ref.at[pl.ds(gi * n * ROW_TILE, n * ROW_TILE), :], hf)
        h_hi = hf.astype(BF16)
        h_lo = (hf - h_hi.astype(F32)).astype(BF16)
        logits.append(jnp.dot(h_hi, wrh_ref[...], preferred_element_type=F32)
                      + jnp.dot(h_lo, wrh_ref[...], preferred_element_type=F32)
                      + jnp.dot(h_hi, wrl_ref[...], preferred_element_type=F32)
                      + br_ref[...])

    lane = lax.broadcasted_iota(I32, (n, LANES), 1).astype(F32)
    row = lax.broadcasted_iota(I32, (n, n), 0)
    col = lax.broadcasted_iota(I32, (n, n), 1)
    lower = jnp.where(row > col, 1.0, 0.0).astype(BF16)
    base = base_ref[0:1, :]
    for gi in range(POST_GROUPS):
        rows = slice(gi * n, (gi + 1) * n)
        work = logits[gi]
        sels, vals, idxs = [], [], []
        for _ in range(TOP_K):
            mx = jnp.max(work, axis=1, keepdims=True)
            idx = jnp.min(jnp.where(work == mx, lane, float(LANES)), axis=1, keepdims=True)
            sel = lane == idx
            work = jnp.where(sel, -jnp.inf, work)
            sels.append(sel)
            vals.append(mx)
            idxs.append(idx)
        es = [jnp.exp(v - vals[0]) for v in vals]
        den = es[0] + es[1] + es[2] + es[3]
        gates = [e / den for e in es]

        multi = jnp.zeros((n, LANES), F32)
        for sel in sels:
            multi = multi + jnp.where(sel, 1.0, 0.0)
        prefix = jnp.dot(lower, multi.astype(BF16), preferred_element_type=F32) + base
        ri = jnp.zeros((n, LANES), F32)
        rg = jnp.zeros((n, LANES), F32)
        for kk in range(TOP_K):
            rank = jnp.sum(jnp.where(sels[kk], prefix, 0.0), axis=1, keepdims=True)
            ri = jnp.where(lane == float(kk), idxs[kk], ri)
            ri = jnp.where(lane == float(TOP_K + kk), rank, ri)
            rg = jnp.where(lane == float(kk), gates[kk], rg)
        ri_ref[rows, :] = ri.astype(I32)
        rg_ref[rows, :] = rg
        base = base + jnp.sum(multi, axis=0, keepdims=True)
    base_ref[0:1, :] = base
    cnt_ref[...] = jnp.broadcast_to(base, cnt_ref.shape)


def _post_call(x2d, o_mla, o_diff, w):
    T = x2d.shape[0]
    tm = TOK_BLOCK
    full = lambda shp: pl.BlockSpec(shp, lambda i: (0,) * len(shp))
    tokb = lambda n: pl.BlockSpec((tm, n), lambda i: (i, 0))
    return pl.pallas_call(
        _post_kernel,
        grid=(T // tm,),
        in_specs=[tokb(D_MODEL), tokb(4 * LANES), tokb(4 * LANES),
                  full((4 * LANES, D_MODEL)), full((4 * LANES, D_MODEL)), full((1, D_MODEL)),
                  full((D_MODEL, LANES)), full((D_MODEL, LANES)), full((1, LANES))],
        out_specs=[tokb(D_MODEL), pl.BlockSpec((tm * ROW_TILE, LANES), lambda i: (i, 0)),
                   tokb(LANES), tokb(LANES), full((8, LANES))],
        out_shape=[jax.ShapeDtypeStruct((T, D_MODEL), F32),
                   jax.ShapeDtypeStruct((T * ROW_TILE, LANES), F32),
                   jax.ShapeDtypeStruct((T, LANES), I32),
                   jax.ShapeDtypeStruct((T, LANES), F32),
                   jax.ShapeDtypeStruct((8, LANES), F32)],
        scratch_shapes=[pltpu.VMEM((8, LANES), F32)],
        compiler_params=_cparams(("arbitrary",)),
        name="post",
    )(x2d, o_mla, o_diff, w["w_out_a"], w["w_out_b"], w["g_ffn"],
      w["w_router_hi"], w["w_router_lo"], w["b_router"])


def _store_rows(ref, val):
    n = val.shape[0]
    for s in range(ROW_TILE):
        ref[pl.ds(s, n, stride=ROW_TILE), :] = val[:, LANES * s:LANES * (s + 1)]


def _load_rows(ref, n):
    return jnp.concatenate([ref[pl.ds(s, n, stride=ROW_TILE), :] for s in range(ROW_TILE)],
                           axis=1)


def _row_copy(src, si, dst, di, sem):
    return pltpu.make_async_copy(
        src.at[pl.ds(pl.multiple_of(si * ROW_TILE, ROW_TILE), ROW_TILE), :],
        dst.at[pl.ds(pl.multiple_of(di * ROW_TILE, ROW_TILE), ROW_TILE), :], sem)


def _wait_rows(src, dst, n_rows, sem):
    for _ in range(0, n_rows, WAIT_ROWS):
        pltpu.make_async_copy(src.at[pl.ds(0, WAIT_ROWS * ROW_TILE), :],
                              dst.at[pl.ds(0, WAIT_ROWS * ROW_TILE), :], sem).wait()


def _dispatch_kernel(pend_ref, npad_ref, dest_ref, hf_ref, xs_ref, zero_ref, sem, zsem):
    tb = dest_ref.shape[1]

    @pl.when(pl.program_id(0) == 0)
    def _():
        zero_ref[...] = jnp.zeros_like(zero_ref)

        def zcopy(e):
            start = pl.multiple_of((pend_ref[e] - MOE_TILE) * ROW_TILE, MOE_TILE * ROW_TILE)
            return pltpu.make_async_copy(
                zero_ref, xs_ref.at[pl.ds(start, MOE_TILE * ROW_TILE), :], zsem)

        def zstart(e, c):
            @pl.when(npad_ref[e] > 0)
            def _():
                zcopy(e).start()
            return c

        def zwait(e, c):
            @pl.when(npad_ref[e] > 0)
            def _():
                zcopy(e).wait()
            return c

        lax.fori_loop(0, N_EXPERTS, zstart, 0)
        lax.fori_loop(0, N_EXPERTS, zwait, 0)

        def tcopy(j):
            start = pl.multiple_of(j * (MOE_TILE * ROW_TILE), MOE_TILE * ROW_TILE)
            return pltpu.make_async_copy(
                zero_ref, xs_ref.at[pl.ds(start, MOE_TILE * ROW_TILE), :], zsem)

        def tstart(j, c):
            tcopy(j).start()
            return c

        def twait(j, c):
            tcopy(j).wait()
            return c

        first_free = pend_ref[N_EXPERTS - 1] // MOE_TILE
        n_blocks = xs_ref.shape[0] // (MOE_TILE * ROW_TILE)
        lax.fori_loop(first_free, n_blocks, tstart, 0)
        lax.fori_loop(first_free, n_blocks, twait, 0)

    def issue(j, c):
        base = pl.multiple_of(j * ISSUE_UNROLL, ISSUE_UNROLL)
        for u in range(ISSUE_UNROLL):
            for kk in range(TOP_K):
                _row_copy(hf_ref, base + u, xs_ref, dest_ref[kk, base + u], sem).start(
                    priority=kk % DMA_THREADS)
        return c

    lax.fori_loop(0, tb // ISSUE_UNROLL, issue, 0)
    _wait_rows(hf_ref, xs_ref, TOP_K * tb, sem)


def _dispatch_call(pad_end, n_pad, dest_t, hf, n_slots):
    T = hf.shape[0] // ROW_TILE
    tb = DISPATCH_BLOCK
    return pl.pallas_call(
        _dispatch_kernel,
        grid_spec=pltpu.PrefetchScalarGridSpec(
            num_scalar_prefetch=2,
            grid=(T // tb,),
            in_specs=[pl.BlockSpec((TOP_K, tb), lambda i, pe, npd: (0, i),
                                   memory_space=pltpu.SMEM),
                      pl.BlockSpec((tb * ROW_TILE, LANES), lambda i, pe, npd: (i, 0))],
            out_specs=pl.BlockSpec(memory_space=pl.ANY),
            scratch_shapes=[pltpu.VMEM((MOE_TILE * ROW_TILE, LANES), F32),
                            pltpu.SemaphoreType.DMA, pltpu.SemaphoreType.DMA]),
        out_shape=jax.ShapeDtypeStruct((n_slots * ROW_TILE, LANES), F32),
        compiler_params=_cparams(("arbitrary",)),
        name="dispatch",
    )(pad_end, n_pad, dest_t, hf)


def _expert_kernel(be_ref, nlive_ref, xs_ref, wg_ref, bg_ref, wu_ref, bu_ref, wd_ref, bd_ref,
                   ys_ref, wgb, wub, wdb):
    i = pl.program_id(0)
    prev = be_ref[jnp.maximum(i - 1, 0)]

    @pl.when(jnp.logical_or(i == 0, be_ref[i] != prev))
    def _():
        wgb[...] = wg_ref[0].astype(BF16)
        wub[...] = wu_ref[0].astype(BF16)
        wdb[...] = wd_ref[0].astype(BF16)

    live = i < nlive_ref[0]

    @pl.when(live)
    def _():
        x = _load_rows(xs_ref, MOE_TILE).astype(BF16)
        g = jnp.dot(x, wgb[...], preferred_element_type=F32) + bg_ref[0]
        u = jnp.dot(x, wub[...], preferred_element_type=F32) + bu_ref[0]
        g = jnp.minimum(g, SWIGLU_LIMIT)
        u = jnp.clip(u, -SWIGLU_LIMIT, SWIGLU_LIMIT)
        a = g * (1.0 / (1.0 + jnp.exp(-SWIGLU_ALPHA * g))) * (u + 1.0)
        y = jnp.dot(a.astype(BF16), wdb[...], preferred_element_type=F32) + bd_ref[0]
        _store_rows(ys_ref, y)

    @pl.when(jnp.logical_not(live))
    def _():
        ys_ref[...] = jnp.zeros_like(ys_ref)


def _expert_call(block_e, n_live, xs, w):
    n_slots = xs.shape[0] // ROW_TILE
    wspec = lambda r, c: pl.BlockSpec((1, r, c), lambda i, be, nl: (be[i], 0, 0))
    slotb = pl.BlockSpec((MOE_TILE * ROW_TILE, LANES), lambda i, be, nl: (i, 0))
    return pl.pallas_call(
        _expert_kernel,
        grid_spec=pltpu.PrefetchScalarGridSpec(
            num_scalar_prefetch=2,
            grid=(n_slots // MOE_TILE,),
            in_specs=[slotb, wspec(D_MODEL, D_FF), wspec(1, D_FF), wspec(D_MODEL, D_FF),
                      wspec(1, D_FF), wspec(D_FF, D_MODEL), wspec(1, D_MODEL)],
            out_specs=slotb,
            scratch_shapes=[pltpu.VMEM((D_MODEL, D_FF), BF16), pltpu.VMEM((D_MODEL, D_FF), BF16),
                            pltpu.VMEM((D_FF, D_MODEL), BF16)]),
        out_shape=jax.ShapeDtypeStruct((n_slots * ROW_TILE, LANES), F32),
        compiler_params=_cparams(("arbitrary",)),
        name="experts",
    )(block_e, n_live, xs, w["w_gate"], w["b_gate"], w["w_up"], w["b_up"], w["w_down"],
      w["b_down"])


def _combine_kernel(dest_ref, dnext_ref, rg_ref, x2_ref, gfin_ref, ys_ref, out_ref, buf, sem,
                    *, final):
    tb = x2_ref.shape[0]
    step = pl.program_id(0)
    slot = lax.rem(step, 2)

    def gather(d_ref, s):
        def issue(j, c):
            base = pl.multiple_of(j * ISSUE_UNROLL, ISSUE_UNROLL)
            for u in range(ISSUE_UNROLL):
                for kk in range(TOP_K):
                    _row_copy(ys_ref, d_ref[kk, base + u], buf.at[s, kk], base + u,
                              sem.at[s]).start(priority=kk % DMA_THREADS)
            return c

        lax.fori_loop(0, tb // ISSUE_UNROLL, issue, 0)

    @pl.when(step == 0)
    def _():
        gather(dest_ref, slot)

    @pl.when(step + 1 < pl.num_programs(0))
    def _():
        gather(dnext_ref, 1 - slot)

    _wait_rows(ys_ref, buf.at[slot, 0], TOP_K * tb, sem.at[slot])

    rg = rg_ref[...]
    y = x2_ref[...]
    for kk in range(TOP_K):
        y = y + _load_rows(buf.at[slot, kk], tb) * rg[:, kk:kk + 1]
    if final:
        y = _rms(y, gfin_ref[...])
    out_ref[...] = y


def _combine_call(dest_t, rg, x2, g_final, ys, *, final):
    T = x2.shape[0]
    tb = COMBINE_BLOCK
    last = T // tb - 1
    return pl.pallas_call(
        functools.partial(_combine_kernel, final=final),
        grid=(T // tb,),
        in_specs=[pl.BlockSpec((TOP_K, tb), lambda i: (0, i), memory_space=pltpu.SMEM),
                  pl.BlockSpec((TOP_K, tb), lambda i: (0, jnp.minimum(i + 1, last)),
                               memory_space=pltpu.SMEM),
                  pl.BlockSpec((tb, LANES), lambda i: (i, 0)),
                  pl.BlockSpec((tb, D_MODEL), lambda i: (i, 0)),
                  pl.BlockSpec((1, D_MODEL), lambda i: (0, 0)),
                  pl.BlockSpec(memory_space=pl.ANY)],
        out_specs=pl.BlockSpec((tb, D_MODEL), lambda i: (i, 0)),
        out_shape=jax.ShapeDtypeStruct((T, D_MODEL), F32),
        scratch_shapes=[pltpu.VMEM((2, TOP_K, tb * ROW_TILE, LANES), F32),
                        pltpu.SemaphoreType.DMA((2,))],
        compiler_params=_cparams(("arbitrary",)),
        name="combine",
    )(dest_t, dest_t, rg, x2, g_final, ys)


def _rope_tables(S, rot_dim, theta, group, lead):
    half = rot_dim // 2
    inv_freq = jnp.power(theta, -2.0 * jnp.arange(half, dtype=F32) / rot_dim)
    ang = jnp.arange(S, dtype=F32)[:, None] * inv_freq[None, :]
    cos = jnp.cos(ang)
    sin = jnp.sin(ang)
    rest = group - lead - rot_dim
    cg = jnp.concatenate([jnp.ones((S, lead), F32), cos, cos, jnp.ones((S, rest), F32)], axis=1)
    sg = jnp.concatenate([jnp.zeros((S, lead), F32), -sin, sin, jnp.zeros((S, rest), F32)], axis=1)
    reps = LANES // group
    return jnp.tile(cg, (1, reps)), jnp.tile(sg, (1, reps))


def _prep_layer(l, S, g_attn, w_in, g_cq, w_uq, g_ckv, w_uk, w_uv, lam_q1, lam_k1, lam_q2,
                lam_k2, g_subln, w_out, g_ffn, w_router, b_router, w_gate, b_gate, w_up, b_up,
                w_down, b_down):
    w = {}
    wi = w_in[l]
    kpe_cols = jnp.zeros((D_MODEL, LANES), F32).at[:, MLA_NOPE:MLA_NOPE + MLA_ROPE].set(
        wi[:, OFF_KPE:OFF_KPE + MLA_ROPE])
    dv0 = OFF_KPE + MLA_ROPE + 2 * DIFF_QK
    w["w_in"] = jnp.concatenate(
        [wi[:, :OFF_KPE], kpe_cols, wi[:, OFF_KPE + MLA_ROPE:dv0]], axis=1).astype(BF16)
    w["w_dv"] = wi[:, dv0:].astype(BF16)
    dqk = MLA_NOPE + MLA_ROPE
    wq = w_uq[l].reshape(Q_LORA, MLA_HEADS, dqk)
    w["w_uq"] = jnp.pad(wq, ((0, 0), (0, 0), (0, HEAD_PAD - dqk))).reshape(
        Q_LORA, MLA_HEADS * HEAD_PAD).astype(BF16)
    wk = w_uk[l].reshape(KV_LORA, MLA_HEADS, MLA_NOPE)
    w["w_uk"] = jnp.pad(wk, ((0, 0), (0, 0), (0, HEAD_PAD - MLA_NOPE))).reshape(
        KV_LORA, MLA_HEADS * HEAD_PAD).astype(BF16)
    w["w_uv"] = w_uv[l].astype(BF16)
    w["g_attn"] = g_attn[l].reshape(1, D_MODEL)
    w["g_cq"] = g_cq[l].reshape(1, Q_LORA)
    w["g_ckv"] = g_ckv[l].reshape(1, KV_LORA)
    w["cos_m"], w["sin_m"] = _rope_tables(S, MLA_ROPE, MLA_THETA, LANES, MLA_NOPE)
    w["cos_d"], w["sin_d"] = _rope_tables(S, DIFF_ROT, DIFF_THETA, DIFF_HD, 0)
    lam = jnp.stack([lam_q1[l], lam_k1[l], lam_q2[l], lam_k2[l]]).astype(F32)
    w["lam"] = jnp.pad(lam, ((0, 8 - 4), (0, LANES - DIFF_HD)))
    w["g_subln"] = g_subln[l].reshape(2 * DIFF_HD, 1)
    wo = w_out[l].astype(BF16)
    w["w_out_a"] = wo[:MLA_HEADS * MLA_V]
    w["w_out_b"] = wo[MLA_HEADS * MLA_V:]
    w["g_ffn"] = g_ffn[l].reshape(1, D_MODEL)
    wr = jnp.pad(w_router[l], ((0, 0), (0, LANES - N_EXPERTS)))
    w["w_router_hi"] = wr.astype(BF16)
    w["w_router_lo"] = (wr - w["w_router_hi"].astype(F32)).astype(BF16)
    w["b_router"] = jnp.pad(b_router[l].astype(F32), (0, LANES - N_EXPERTS),
                            constant_values=-jnp.inf).reshape(1, LANES)
    w["w_gate"], w["w_up"], w["w_down"] = w_gate[l], w_up[l], w_down[l]
    w["b_gate"] = b_gate[l].reshape(N_EXPERTS, 1, D_FF)
    w["b_up"] = b_up[l].reshape(N_EXPERTS, 1, D_FF)
    w["b_down"] = b_down[l].reshape(N_EXPERTS, 1, D_MODEL)
    w["lambda_init"] = 0.8 - 0.6 * math.exp(-0.3 * l)
    return w


def _layer(x2d, B, S, w, g_final, *, final):
    T = x2d.shape[0]
    q, k, vt, dq, dk, dvt = _proj_call(x2d, S, w)
    o_mla = _attn_call(q, k, vt, B, S, mla=True)
    o_diff = _attn_call(dq, dk, dvt, B, S, mla=False, lam=w["lam"], g=w["g_subln"],
                        lambda_init=w["lambda_init"])
    x2, hf, ri, rg, cnt = _post_call(x2d, o_mla, o_diff, w)

    counts = cnt[0, :N_EXPERTS].astype(I32)
    padded = ((counts + MOE_TILE - 1) // MOE_TILE) * MOE_TILE
    pad_end = jnp.cumsum(padded)
    pad_start = pad_end - padded
    idx = ri[:, :TOP_K]
    dest = pad_start[idx] + ri[:, TOP_K:2 * TOP_K]
    dest_t = dest.T
    n_blocks = (T * TOP_K) // MOE_TILE + N_EXPERTS
    n_slots = n_blocks * MOE_TILE
    blk_start = jnp.arange(n_blocks, dtype=I32) * MOE_TILE
    block_e = jnp.minimum(jnp.sum((pad_end[None, :] <= blk_start[:, None]).astype(I32), axis=1),
                          N_EXPERTS - 1).astype(I32)
    xs = _dispatch_call(pad_end.astype(I32), (padded - counts).astype(I32), dest_t, hf, n_slots)
    n_live = (pad_end[N_EXPERTS - 1:] // MOE_TILE).astype(I32)
    ys = _expert_call(block_e, n_live, xs, w)
    return _combine_call(dest_t, rg, x2, g_final, ys, final=final)


def kernel(x_prompt, x_sample, g_attn, w_in, g_cq, w_uq, g_ckv, w_uk, w_uv, lam_q1, lam_k1,
           lam_q2, lam_k2, g_subln, w_out, g_ffn, w_router, b_router, w_gate, b_gate, w_up, b_up,
           w_down, b_down, g_final):
    depth = w_in.shape[0]
    gfin = g_final.reshape(1, D_MODEL)

    def trunk(x):
        B, S, D = x.shape
        assert D == D_MODEL and S % TOK_BLOCK == 0
        x2d = x.reshape(B * S, D)
        for l in range(depth):
            w = _prep_layer(l, S, g_attn, w_in, g_cq, w_uq, g_ckv, w_uk, w_uv, lam_q1, lam_k1,
                            lam_q2, lam_k2, g_subln, w_out, g_ffn, w_router, b_router, w_gate,
                            b_gate, w_up, b_up, w_down, b_down)
            x2d = _layer(x2d, B, S, w, gfin, final=(l == depth - 1))
        return x2d.reshape(B, S, D)

    return (trunk(x_prompt), trunk(x_sample))
```

```python
import functools
import math

import jax
import jax.numpy as jnp
from jax import lax
from jax.experimental import pallas as pl
from jax.experimental.pallas import tpu as pltpu

F32 = jnp.float32
BF16 = jnp.bfloat16
I32 = jnp.int32

D_MODEL = 1024
MLA_HEADS = 8
MLA_NOPE = 64
MLA_ROPE = 32
MLA_V = 64
Q_LORA = 384
KV_LORA = 256
MLA_THETA = 10000.0
DIFF_HEADS = 4
DIFF_HD = 64
DIFF_ROT = DIFF_HD // 4
DIFF_THETA = 500000.0
DIFF_QK = DIFF_HEADS * 2 * DIFF_HD
DIFF_VW = DIFF_HEADS * 2 * DIFF_HD
N_EXPERTS = 32
TOP_K = 4
D_FF = 1024
SWIGLU_ALPHA = 1.702
SWIGLU_LIMIT = 7.0
RMS_EPS = 1e-5

LANES = 128
SUBLANES = 8
HEAD_PAD = 128
P_IN_PAD = Q_LORA + KV_LORA + LANES + 2 * DIFF_QK
OFF_CKV = Q_LORA
OFF_KPE = Q_LORA + KV_LORA
OFF_DQ = OFF_KPE + LANES
OFF_DK = OFF_DQ + DIFF_QK
LOG2_E = math.log2(math.e)
MLA_SCALE = LOG2_E / math.sqrt(MLA_NOPE + MLA_ROPE)
DIFF_SCALE = LOG2_E / math.sqrt(DIFF_HD)

TOK_BLOCK = 512
Q_TILE = 1024
Q_SUB = 256
KEY_CHUNK = 256
SUB_INTERLEAVE = 1
MOE_TILE = 512
POST_GROUPS = 1
ROW_TILE = D_MODEL // LANES
DISPATCH_BLOCK = 1024
COMBINE_BLOCK = 256
ISSUE_UNROLL = 8
DMA_THREADS = 2
WAIT_ROWS = 128
SCORE_LOOKAHEAD = 8
ONES_ROWS = 16
VMEM_LIMIT = 56 * 1024 * 1024


def _rms(x, g):
    ms = jnp.mean(x * x, axis=-1, keepdims=True)
    return x * lax.rsqrt(ms + RMS_EPS) * g


def _cparams(sem):
    return pltpu.CompilerParams(dimension_semantics=sem, vmem_limit_bytes=VMEM_LIMIT)


def _proj_kernel(x_ref, ga_ref, win_ref, wdv_ref, gcq_ref, wuq_ref, gckv_ref, wuk_ref, wuv_ref,
                 cm_ref, sm_ref, cd_ref, sd_ref,
                 q_ref, k_ref, vt_ref, dq_ref, dk_ref, dvt_ref):
    x = x_ref[...]
    tm = x.shape[0]
    h = _rms(x, ga_ref[...]).astype(BF16)
    proj = jnp.dot(h, win_ref[...], preferred_element_type=F32)
    ckv = _rms(proj[:, OFF_CKV:OFF_KPE], gckv_ref[...]).astype(BF16)

    lane = lax.broadcasted_iota(I32, (tm, LANES), 1)
    first_m = (lane & (MLA_ROPE // 2)) == 0
    first_d = (lane & (DIFF_ROT // 2)) == 0
    cm, sm, cd, sd = cm_ref[...], sm_ref[...], cd_ref[...], sd_ref[...]

    def rope_m(v):
        partner = jnp.where(first_m, pltpu.roll(v, LANES - MLA_ROPE // 2, 1),
                            pltpu.roll(v, MLA_ROPE // 2, 1))
        return v * cm + partner * sm

    def rope_d(v):
        partner = jnp.where(first_d, pltpu.roll(v, LANES - DIFF_ROT // 2, 1),
                            pltpu.roll(v, DIFF_ROT // 2, 1))
        return v * cd + partner * sd

    cq = _rms(proj[:, 0:Q_LORA], gcq_ref[...]).astype(BF16)
    q = jnp.dot(cq, wuq_ref[...], preferred_element_type=F32)
    kn = jnp.dot(ckv, wuk_ref[...], preferred_element_type=F32)
    kpe = rope_m(proj[:, OFF_KPE:OFF_DQ])
    for hh in range(MLA_HEADS):
        sl = slice(HEAD_PAD * hh, HEAD_PAD * (hh + 1))
        q_ref[:, sl] = (rope_m(q[:, sl]) * MLA_SCALE).astype(BF16)
        k_ref[:, sl] = (kn[:, sl] + kpe).astype(BF16)

    for i in range(DIFF_QK // LANES):
        sl = slice(LANES * i, LANES * (i + 1))
        dq_ref[:, sl] = (rope_d(proj[:, OFF_DQ + LANES * i:OFF_DQ + LANES * (i + 1)])
                         * DIFF_SCALE).astype(BF16)
        dk_ref[:, sl] = rope_d(proj[:, OFF_DK + LANES * i:OFF_DK + LANES * (i + 1)]).astype(BF16)
    vt_ref[0] = jnp.dot(ckv, wuv_ref[...], preferred_element_type=F32).T.astype(BF16)
    dvt_ref[0] = jnp.dot(h, wdv_ref[...], preferred_element_type=F32).T.astype(BF16)


def _proj_call(x2d, S, w):
    T = x2d.shape[0]
    tm = TOK_BLOCK
    nblk = T // tm
    spb = S // tm
    full = lambda shp: pl.BlockSpec(shp, lambda i: (0,) * len(shp))
    tab = pl.BlockSpec((tm, LANES), lambda i: (i % spb, 0))
    tokb = lambda n: pl.BlockSpec((tm, n), lambda i: (i, 0))
    vtb = pl.BlockSpec((1, DIFF_VW, tm), lambda i: (i, 0, 0))
    return pl.pallas_call(
        _proj_kernel,
        grid=(nblk,),
        in_specs=[tokb(D_MODEL), full((1, D_MODEL)), full((D_MODEL, P_IN_PAD)),
                  full((D_MODEL, DIFF_VW)),
                  full((1, Q_LORA)), full((Q_LORA, MLA_HEADS * HEAD_PAD)),
                  full((1, KV_LORA)), full((KV_LORA, MLA_HEADS * HEAD_PAD)),
                  full((KV_LORA, MLA_HEADS * MLA_V)), tab, tab, tab, tab],
        out_specs=[tokb(MLA_HEADS * HEAD_PAD), tokb(MLA_HEADS * HEAD_PAD), vtb,
                   tokb(DIFF_QK), tokb(DIFF_QK), vtb],
        out_shape=[jax.ShapeDtypeStruct((T, MLA_HEADS * HEAD_PAD), BF16),
                   jax.ShapeDtypeStruct((T, MLA_HEADS * HEAD_PAD), BF16),
                   jax.ShapeDtypeStruct((nblk, MLA_HEADS * MLA_V, tm), BF16),
                   jax.ShapeDtypeStruct((T, DIFF_QK), BF16),
                   jax.ShapeDtypeStruct((T, DIFF_QK), BF16),
                   jax.ShapeDtypeStruct((nblk, DIFF_VW, tm), BF16)],
        compiler_params=_cparams(("arbitrary",)),
        name="proj",
    )(x2d, w["g_attn"], w["w_in"], w["w_dv"], w["g_cq"], w["w_uq"], w["g_ckv"], w["w_uk"],
      w["w_uv"], w["cos_m"], w["sin_m"], w["cos_d"], w["sin_d"])


def _attn_kernel(*refs, mla, lambda_init):
    if mla:
        q_ref, k_ref, vt_ref, o_ref = refs
    else:
        lam_ref, g_ref, q_ref, k_ref, vt_ref, o_ref = refs
    dv = MLA_V if mla else 2 * DIFF_HD
    kc = KEY_CHUNK
    vt_blk = vt_ref.shape[2]
    n_chunks = k_ref.shape[0] // kc
    ones = jnp.ones((ONES_ROWS, kc), BF16)
    if not mla:
        a = lam_ref[...]
        s1 = jnp.sum(a[0:1, :] * a[1:2, :], axis=1, keepdims=True)
        s2 = jnp.sum(a[2:3, :] * a[3:4, :], axis=1, keepdims=True)
        lam = jnp.exp(s1) - jnp.exp(s2) + lambda_init

    def query(t, m):
        rows = slice(Q_SUB * t, Q_SUB * (t + 1))
        if mla:
            return q_ref[rows, HEAD_PAD * m:HEAD_PAD * (m + 1)]
        lane = lax.broadcasted_iota(I32, (Q_SUB, LANES), 1)
        keep = (lane < DIFF_HD) if m == 0 else (lane >= DIFF_HD)
        qf = q_ref[rows, :]
        return jnp.where(keep, qf, jnp.zeros_like(qf))

    def scores(t, c, m):
        if mla:
            kb = k_ref[kc * c:kc * (c + 1), HEAD_PAD * m:HEAD_PAD * (m + 1)]
        else:
            kb = k_ref[kc * c:kc * (c + 1), :]
        return lax.dot_general(kb, qs[t][m], (((1,), (1,)), ((), ())),
                               preferred_element_type=F32)

    def finish(t, acc):
        outs = [acc[m][0:dv] / acc[m][dv:dv + 1] for m in range(2)]
        if mla:
            o = jnp.concatenate(outs, axis=0)
        else:
            o = outs[0] - lam * outs[1]
            ms = jnp.mean(o * o, axis=0, keepdims=True)
            o = o * lax.rsqrt(ms + RMS_EPS) * g_ref[...] * (1.0 - lambda_init)
        o_ref[Q_SUB * t:Q_SUB * (t + 1), :] = o.T.astype(BF16)

    n_sub = q_ref.shape[0] // Q_SUB
    qs = [[query(t, m) for m in range(2)] for t in range(n_sub)]
    order = [(t, c, m) for tp in range(0, n_sub, SUB_INTERLEAVE) for c in range(n_chunks)
             for t in range(tp, tp + SUB_INTERLEAVE) for m in range(2)]
    pending = [scores(*order[i]) for i in range(min(SCORE_LOOKAHEAD, len(order)))]
    m_all = [[jnp.full((1, Q_SUB), -jnp.inf, F32) for _ in range(2)] for _ in range(n_sub)]
    acc_all = [[jnp.zeros((dv + ONES_ROWS, Q_SUB), F32) for _ in range(2)] for _ in range(n_sub)]
    for i, (t, c, m) in enumerate(order):
        m_i, acc = m_all[t], acc_all[t]
        s = pending.pop(0)
        if i + SCORE_LOOKAHEAD < len(order):
            pending.append(scores(*order[i + SCORE_LOOKAHEAD]))
        vrows = slice(MLA_V * m, MLA_V * (m + 1)) if mla else slice(0, dv)
        vcols = slice((kc * c) % vt_blk, (kc * c) % vt_blk + kc)
        vb = jnp.concatenate([vt_ref[(kc * c) // vt_blk, vrows, vcols], ones], axis=0)
        m_new = jnp.maximum(m_i[m], jnp.max(s, axis=0, keepdims=True))
        alpha = jnp.exp2(m_i[m] - m_new)
        p = jnp.exp2(s - m_new).astype(BF16)
        acc[m] = alpha * acc[m] + jnp.dot(vb, p, preferred_element_type=F32)
        m_i[m] = m_new
        if c == n_chunks - 1 and m == 1:
            finish(t, acc)


def _attn_call(q, k, vt, B, S, *, mla, lam=None, g=None, lambda_init=0.0):
    T = q.shape[0]
    tq = min(Q_TILE, S)
    nq = S // tq
    vt_blk = vt.shape[2]
    assert vt_blk % KEY_CHUNK == 0 and S % vt_blk == 0
    qw = 2 * HEAD_PAD if mla else LANES
    in_specs = [pl.BlockSpec((tq, qw), lambda b, j, i: (b * nq + i, j)),
                pl.BlockSpec((S, qw), lambda b, j, i: (b, j)),
                pl.BlockSpec((S // vt_blk, LANES, vt_blk), lambda b, j, i: (b, j, 0))]
    args = [q, k, vt]
    if not mla:
        in_specs = [pl.BlockSpec((SUBLANES, LANES), lambda b, j, i: (0, 0)),
                    pl.BlockSpec((2 * DIFF_HD, 1), lambda b, j, i: (0, 0))] + in_specs
        args = [lam, g] + args
    return pl.pallas_call(
        functools.partial(_attn_kernel, mla=mla, lambda_init=lambda_init),
        grid=(B, 4, nq),
        in_specs=in_specs,
        out_specs=pl.BlockSpec((tq, LANES), lambda b, j, i: (b * nq + i, j)),
        out_shape=jax.ShapeDtypeStruct((T, 4 * LANES), BF16),
        compiler_params=_cparams(("arbitrary", "arbitrary", "arbitrary")),
        name="attn_mla" if mla else "attn_diff",
    )(*args)


def _post_kernel(x_ref, om_ref, od_ref, woa_ref, wob_ref, gf_ref, wrh_ref, wrl_ref, br_ref,
                 x2_ref, hf_ref, ri_ref, rg_ref, cnt_ref, base_ref):
    step = pl.program_id(0)

    @pl.when(step == 0)
    def _():
        base_ref[...] = jnp.zeros_like(base_ref)

    n = x_ref.shape[0] // POST_GROUPS
    logits = []
    for gi in range(POST_GROUPS):
        rows = slice(gi * n, (gi + 1) * n)
        x2 = (x_ref[rows, :]
              + jnp.dot(om_ref[rows, :], woa_ref[...], preferred_element_type=F32)
              + jnp.dot(od_ref[rows, :], wob_ref[...], preferred_element_type=F32))
        x2_ref[rows, :] = x2
        hf = _rms(x2, gf_ref[...])
        _store_rows(hf_ref.at[pl.ds(gi * n * ROW_TILE, n * ROW_TILE), :], hf)
        h_hi = hf.astype(BF16)
        h_lo = (hf - h_hi.astype(F32)).astype(BF16)
        logits.append(jnp.dot(h_hi, wrh_ref[...], preferred_element_type=F32)
                      + jnp.dot(h_lo, wrh_ref[...], preferred_element_type=F32)
                      + jnp.dot(h_hi, wrl_ref[...], preferred_element_type=F32)
                      + br_ref[...])

    lane = lax.broadcasted_iota(I32, (n, LANES), 1).astype(F32)
    row = lax.broadcasted_iota(I32, (n, n), 0)
    col = lax.broadcasted_iota(I32, (n, n), 1)
    lower = jnp.where(row > col, 1.0, 0.0).astype(BF16)
    base = base_ref[0:1, :]
    for gi in range(POST_GROUPS):
        rows = slice(gi * n, (gi + 1) * n)
        work = logits[gi]
        sels, vals, idxs = [], [], []
        for _ in range(TOP_K):
            mx = jnp.max(work, axis=1, keepdims=True)
            idx = jnp.min(jnp.where(work == mx, lane, float(LANES)), axis=1, keepdims=True)
            sel = lane == idx
            work = jnp.where(sel, -jnp.inf, work)
            sels.append(sel)
            vals.append(mx)
            idxs.append(idx)
        es = [jnp.exp(v - vals[0]) for v in vals]
        den = es[0] + es[1] + es[2] + es[3]
        gates = [e / den for e in es]

        multi = jnp.zeros((n, LANES), F32)
        for sel in sels:
            multi = multi + jnp.where(sel, 1.0, 0.0)
        prefix = jnp.dot(lower, multi.astype(BF16), preferred_element_type=F32) + base
        ri = jnp.zeros((n, LANES), F32)
        rg = jnp.zeros((n, LANES), F32)
        for kk in range(TOP_K):
            rank = jnp.sum(jnp.where(sels[kk], prefix, 0.0), axis=1, keepdims=True)
            ri = jnp.where(lane == float(kk), idxs[kk], ri)
            ri = jnp.where(lane == float(TOP_K + kk), rank, ri)
            rg = jnp.where(lane == float(kk), gates[kk], rg)
        ri_ref[rows, :] = ri.astype(I32)
        rg_ref[rows, :] = rg
        base = base + jnp.sum(multi, axis=0, keepdims=True)
    base_ref[0:1, :] = base
    cnt_ref[...] = jnp.broadcast_to(base, cnt_ref.shape)


def _post_call(x2d, o_mla, o_diff, w):
    T = x2d.shape[0]
    tm = TOK_BLOCK
    full = lambda shp: pl.BlockSpec(shp, lambda i: (0,) * len(shp))
    tokb = lambda n: pl.BlockSpec((tm, n), lambda i: (i, 0))
    return pl.pallas_call(
        _post_kernel,
        grid=(T // tm,),
        in_specs=[tokb(D_MODEL), tokb(4 * LANES), tokb(4 * LANES),
                  full((4 * LANES, D_MODEL)), full((4 * LANES, D_MODEL)), full((1, D_MODEL)),
                  full((D_MODEL, LANES)), full((D_MODEL, LANES)), full((1, LANES))],
        out_specs=[tokb(D_MODEL), pl.BlockSpec((tm * ROW_TILE, LANES), lambda i: (i, 0)),
                   tokb(LANES), tokb(LANES), full((SUBLANES, LANES))],
        out_shape=[jax.ShapeDtypeStruct((T, D_MODEL), F32),
                   jax.ShapeDtypeStruct((T * ROW_TILE, LANES), F32),
                   jax.ShapeDtypeStruct((T, LANES), I32),
                   jax.ShapeDtypeStruct((T, LANES), F32),
                   jax.ShapeDtypeStruct((SUBLANES, LANES), F32)],
        scratch_shapes=[pltpu.VMEM((SUBLANES, LANES), F32)],
        compiler_params=_cparams(("arbitrary",)),
        name="post",
    )(x2d, o_mla, o_diff, w["w_out_a"], w["w_out_b"], w["g_ffn"],
      w["w_router_hi"], w["w_router_lo"], w["b_router"])


def _store_rows(ref, val):
    n = val.shape[0]
    for s in range(ROW_TILE):
        ref[pl.ds(s, n, stride=ROW_TILE), :] = val[:, LANES * s:LANES * (s + 1)]


def _load_rows(ref, n):
    return jnp.concatenate([ref[pl.ds(s, n, stride=ROW_TILE), :] for s in range(ROW_TILE)],
                           axis=1)


def _row_copy(src, si, dst, di, sem):
    return pltpu.make_async_copy(
        src.at[pl.ds(pl.multiple_of(si * ROW_TILE, ROW_TILE), ROW_TILE), :],
        dst.at[pl.ds(pl.multiple_of(di * ROW_TILE, ROW_TILE), ROW_TILE), :], sem)


def _wait_rows(src, dst, n_rows, sem):
    for _ in range(0, n_rows, WAIT_ROWS):
        pltpu.make_async_copy(src.at[pl.ds(0, WAIT_ROWS * ROW_TILE), :],
                              dst.at[pl.ds(0, WAIT_ROWS * ROW_TILE), :], sem).wait()


def _dispatch_kernel(pend_ref, npad_ref, dest_ref, hf_ref, xs_ref, zero_ref, sem, zsem):
    tb = dest_ref.shape[0] // TOP_K

    @pl.when(pl.program_id(0) == 0)
    def _():
        zero_ref[...] = jnp.zeros_like(zero_ref)

        def zcopy(e):
            start = pl.multiple_of((pend_ref[e] - MOE_TILE) * ROW_TILE, MOE_TILE * ROW_TILE)
            return pltpu.make_async_copy(
                zero_ref, xs_ref.at[pl.ds(start, MOE_TILE * ROW_TILE), :], zsem)

        def zstart(e, c):
            @pl.when(npad_ref[e] > 0)
            def _():
                zcopy(e).start()
            return c

        def zwait(e, c):
            @pl.when(npad_ref[e] > 0)
            def _():
                zcopy(e).wait()
            return c

        lax.fori_loop(0, N_EXPERTS, zstart, 0)
        lax.fori_loop(0, N_EXPERTS, zwait, 0)

        def tcopy(j):
            start = pl.multiple_of(j * (MOE_TILE * ROW_TILE), MOE_TILE * ROW_TILE)
            return pltpu.make_async_copy(
                zero_ref, xs_ref.at[pl.ds(start, MOE_TILE * ROW_TILE), :], zsem)

        def tstart(j, c):
            tcopy(j).start()
            return c

        def twait(j, c):
            tcopy(j).wait()
            return c

        first_free = pend_ref[N_EXPERTS - 1] // MOE_TILE
        n_blocks = xs_ref.shape[0] // (MOE_TILE * ROW_TILE)
        lax.fori_loop(first_free, n_blocks, tstart, 0)
        lax.fori_loop(first_free, n_blocks, twait, 0)

    def issue(j, c):
        base = pl.multiple_of(j * ISSUE_UNROLL, ISSUE_UNROLL)
        for u in range(ISSUE_UNROLL):
            for kk in range(TOP_K):
                _row_copy(hf_ref, base + u, xs_ref, dest_ref[(base + u) * TOP_K + kk],
                          sem).start(priority=kk % DMA_THREADS)
        return c

    lax.fori_loop(0, tb // ISSUE_UNROLL, issue, 0)
    _wait_rows(hf_ref, xs_ref, TOP_K * tb, sem)


def _dispatch_call(pad_end, n_pad, dest_t, hf, n_slots):
    T = hf.shape[0] // ROW_TILE
    tb = DISPATCH_BLOCK
    return pl.pallas_call(
        _dispatch_kernel,
        grid_spec=pltpu.PrefetchScalarGridSpec(
            num_scalar_prefetch=2,
            grid=(T // tb,),
            in_specs=[pl.BlockSpec((tb * TOP_K,), lambda i, pe, npd: (i,),
                                   memory_space=pltpu.SMEM),
                      pl.BlockSpec((tb * ROW_TILE, LANES), lambda i, pe, npd: (i, 0))],
            out_specs=pl.BlockSpec(memory_space=pl.ANY),
            scratch_shapes=[pltpu.VMEM((MOE_TILE * ROW_TILE, LANES), F32),
                            pltpu.SemaphoreType.DMA, pltpu.SemaphoreType.DMA]),
        out_shape=jax.ShapeDtypeStruct((n_slots * ROW_TILE, LANES), F32),
        compiler_params=_cparams(("arbitrary",)),
        name="dispatch",
    )(pad_end, n_pad, dest_t, hf)


def _expert_kernel(be_ref, nlive_ref, xs_ref, wg_ref, bg_ref, wu_ref, bu_ref, wd_ref, bd_ref,
                   ys_ref, wgb, wub, wdb):
    i = pl.program_id(0)
    prev = be_ref[jnp.maximum(i - 1, 0)]

    @pl.when(jnp.logical_or(i == 0, be_ref[i] != prev))
    def _():
        wgb[...] = wg_ref[0].astype(BF16)
        wub[...] = wu_ref[0].astype(BF16)
        wdb[...] = wd_ref[0].astype(BF16)

    live = i < nlive_ref[0]

    @pl.when(live)
    def _():
        x = _load_rows(xs_ref, MOE_TILE).astype(BF16)
        g = jnp.dot(x, wgb[...], preferred_element_type=F32) + bg_ref[0]
        u = jnp.dot(x, wub[...], preferred_element_type=F32) + bu_ref[0]
        g = jnp.minimum(g, SWIGLU_LIMIT)
        u = jnp.clip(u, -SWIGLU_LIMIT, SWIGLU_LIMIT)
        a = g * (1.0 / (1.0 + jnp.exp(-SWIGLU_ALPHA * g))) * (u + 1.0)
        y = jnp.dot(a.astype(BF16), wdb[...], preferred_element_type=F32) + bd_ref[0]
        _store_rows(ys_ref, y)

    @pl.when(jnp.logical_not(live))
    def _():
        ys_ref[...] = jnp.zeros_like(ys_ref)


def _expert_call(block_e, n_live, xs, w):
    n_slots = xs.shape[0] // ROW_TILE
    wspec = lambda r, c: pl.BlockSpec((1, r, c), lambda i, be, nl: (be[i], 0, 0))
    slotb = pl.BlockSpec((MOE_TILE * ROW_TILE, LANES), lambda i, be, nl: (i, 0))
    return pl.pallas_call(
        _expert_kernel,
        grid_spec=pltpu.PrefetchScalarGridSpec(
            num_scalar_prefetch=2,
            grid=(n_slots // MOE_TILE,),
            in_specs=[slotb, wspec(D_MODEL, D_FF), wspec(1, D_FF), wspec(D_MODEL, D_FF),
                      wspec(1, D_FF), wspec(D_FF, D_MODEL), wspec(1, D_MODEL)],
            out_specs=slotb,
            scratch_shapes=[pltpu.VMEM((D_MODEL, D_FF), BF16), pltpu.VMEM((D_MODEL, D_FF), BF16),
                            pltpu.VMEM((D_FF, D_MODEL), BF16)]),
        out_shape=jax.ShapeDtypeStruct((n_slots * ROW_TILE, LANES), F32),
        compiler_params=_cparams(("arbitrary",)),
        name="experts",
    )(block_e, n_live, xs, w["w_gate"], w["b_gate"], w["w_up"], w["b_up"], w["w_down"],
      w["b_down"])


def _combine_kernel(dest_ref, dnext_ref, rg_ref, x2_ref, gfin_ref, ys_ref, out_ref, buf, sem,
                    *, final):
    tb = x2_ref.shape[0]
    step = pl.program_id(0)
    slot = lax.rem(step, 2)

    def gather(d_ref, s):
        def issue(j, c):
            base = pl.multiple_of(j * ISSUE_UNROLL, ISSUE_UNROLL)
            for u in range(ISSUE_UNROLL):
                for kk in range(TOP_K):
                    _row_copy(ys_ref, d_ref[(base + u) * TOP_K + kk], buf.at[s, kk], base + u,
                              sem.at[s]).start(priority=kk % DMA_THREADS)
            return c

        lax.fori_loop(0, tb // ISSUE_UNROLL, issue, 0)

    @pl.when(step == 0)
    def _():
        gather(dest_ref, slot)

    @pl.when(step + 1 < pl.num_programs(0))
    def _():
        gather(dnext_ref, 1 - slot)

    _wait_rows(ys_ref, buf.at[slot, 0], TOP_K * tb, sem.at[slot])

    rg = rg_ref[...]
    y = x2_ref[...]
    for kk in range(TOP_K):
        y = y + _load_rows(buf.at[slot, kk], tb) * rg[:, kk:kk + 1]
    if final:
        y = _rms(y, gfin_ref[...])
    out_ref[...] = y


def _combine_call(dest_t, rg, x2, g_final, ys, *, final):
    T = x2.shape[0]
    tb = COMBINE_BLOCK
    last = T // tb - 1
    return pl.pallas_call(
        functools.partial(_combine_kernel, final=final),
        grid=(T // tb,),
        in_specs=[pl.BlockSpec((tb * TOP_K,), lambda i: (i,), memory_space=pltpu.SMEM),
                  pl.BlockSpec((tb * TOP_K,), lambda i: (jnp.minimum(i + 1, last),),
                               memory_space=pltpu.SMEM),
                  pl.BlockSpec((tb, LANES), lambda i: (i, 0)),
                  pl.BlockSpec((tb, D_MODEL), lambda i: (i, 0)),
                  pl.BlockSpec((1, D_MODEL), lambda i: (0, 0)),
                  pl.BlockSpec(memory_space=pl.ANY)],
        out_specs=pl.BlockSpec((tb, D_MODEL), lambda i: (i, 0)),
        out_shape=jax.ShapeDtypeStruct((T, D_MODEL), F32),
        scratch_shapes=[pltpu.VMEM((2, TOP_K, tb * ROW_TILE, LANES), F32),
                        pltpu.SemaphoreType.DMA((2,))],
        compiler_params=_cparams(("arbitrary",)),
        name="combine",
    )(dest_t, dest_t, rg, x2, g_final, ys)


def _rope_tables(S, rot_dim, theta, group, lead):
    half = rot_dim // 2
    inv_freq = jnp.power(theta, -2.0 * jnp.arange(half, dtype=F32) / rot_dim)
    ang = jnp.arange(S, dtype=F32)[:, None] * inv_freq[None, :]
    cos = jnp.cos(ang)
    sin = jnp.sin(ang)
    rest = group - lead - rot_dim
    cg = jnp.concatenate([jnp.ones((S, lead), F32), cos, cos, jnp.ones((S, rest), F32)], axis=1)
    sg = jnp.concatenate([jnp.zeros((S, lead), F32), -sin, sin, jnp.zeros((S, rest), F32)], axis=1)
    reps = LANES // group
    return jnp.tile(cg, (1, reps)), jnp.tile(sg, (1, reps))


def _prep_layer(l, S, g_attn, w_in, g_cq, w_uq, g_ckv, w_uk, w_uv, lam_q1, lam_k1, lam_q2,
                lam_k2, g_subln, w_out, g_ffn, w_router, b_router, w_gate, b_gate, w_up, b_up,
                w_down, b_down):
    w = {}
    wi = w_in[l]
    kpe_cols = jnp.zeros((D_MODEL, LANES), F32).at[:, MLA_NOPE:MLA_NOPE + MLA_ROPE].set(
        wi[:, OFF_KPE:OFF_KPE + MLA_ROPE])
    dv0 = OFF_KPE + MLA_ROPE + 2 * DIFF_QK
    w["w_in"] = jnp.concatenate(
        [wi[:, :OFF_KPE], kpe_cols, wi[:, OFF_KPE + MLA_ROPE:dv0]], axis=1).astype(BF16)
    w["w_dv"] = wi[:, dv0:].astype(BF16)
    dqk = MLA_NOPE + MLA_ROPE
    wq = w_uq[l].reshape(Q_LORA, MLA_HEADS, dqk)
    w["w_uq"] = jnp.pad(wq, ((0, 0), (0, 0), (0, HEAD_PAD - dqk))).reshape(
        Q_LORA, MLA_HEADS * HEAD_PAD).astype(BF16)
    wk = w_uk[l].reshape(KV_LORA, MLA_HEADS, MLA_NOPE)
    w["w_uk"] = jnp.pad(wk, ((0, 0), (0, 0), (0, HEAD_PAD - MLA_NOPE))).reshape(
        KV_LORA, MLA_HEADS * HEAD_PAD).astype(BF16)
    w["w_uv"] = w_uv[l].astype(BF16)
    w["g_attn"] = g_attn[l].reshape(1, D_MODEL)
    w["g_cq"] = g_cq[l].reshape(1, Q_LORA)
    w["g_ckv"] = g_ckv[l].reshape(1, KV_LORA)
    w["cos_m"], w["sin_m"] = _rope_tables(S, MLA_ROPE, MLA_THETA, LANES, MLA_NOPE)
    w["cos_d"], w["sin_d"] = _rope_tables(S, DIFF_ROT, DIFF_THETA, DIFF_HD, 0)
    lam = jnp.stack([lam_q1[l], lam_k1[l], lam_q2[l], lam_k2[l]]).astype(F32)
    w["lam"] = jnp.pad(lam, ((0, SUBLANES - lam.shape[0]), (0, LANES - DIFF_HD)))
    w["g_subln"] = g_subln[l].reshape(2 * DIFF_HD, 1)
    wo = w_out[l].astype(BF16)
    w["w_out_a"] = wo[:MLA_HEADS * MLA_V]
    w["w_out_b"] = wo[MLA_HEADS * MLA_V:]
    w["g_ffn"] = g_ffn[l].reshape(1, D_MODEL)
    wr = jnp.pad(w_router[l], ((0, 0), (0, LANES - N_EXPERTS)))
    w["w_router_hi"] = wr.astype(BF16)
    w["w_router_lo"] = (wr - w["w_router_hi"].astype(F32)).astype(BF16)
    w["b_router"] = jnp.pad(b_router[l].astype(F32), (0, LANES - N_EXPERTS),
                            constant_values=-jnp.inf).reshape(1, LANES)
    w["w_gate"], w["w_up"], w["w_down"] = w_gate[l], w_up[l], w_down[l]
    w["b_gate"] = b_gate[l].reshape(N_EXPERTS, 1, D_FF)
    w["b_up"] = b_up[l].reshape(N_EXPERTS, 1, D_FF)
    w["b_down"] = b_down[l].reshape(N_EXPERTS, 1, D_MODEL)
    w["lambda_init"] = 0.8 - 0.6 * math.exp(-0.3 * l)
    return w


def _layer(x2d, B, S, w, g_final, *, final):
    T = x2d.shape[0]
    q, k, vt, dq, dk, dvt = _proj_call(x2d, S, w)
    o_mla = _attn_call(q, k, vt, B, S, mla=True)
    o_diff = _attn_call(dq, dk, dvt, B, S, mla=False, lam=w["lam"], g=w["g_subln"],
                        lambda_init=w["lambda_init"])
    x2, hf, ri, rg, cnt = _post_call(x2d, o_mla, o_diff, w)

    counts = cnt[0, :N_EXPERTS].astype(I32)
    padded = ((counts + MOE_TILE - 1) // MOE_TILE) * MOE_TILE
    pad_end = jnp.cumsum(padded)
    pad_start = pad_end - padded
    idx = ri[:, :TOP_K].reshape(-1)
    dest_t = pad_start[idx] + ri[:, TOP_K:2 * TOP_K].reshape(-1)
    n_blocks = (T * TOP_K) // MOE_TILE + N_EXPERTS
    n_slots = n_blocks * MOE_TILE
    blk_start = jnp.arange(n_blocks, dtype=I32) * MOE_TILE
    block_e = jnp.minimum(jnp.sum((pad_end[None, :] <= blk_start[:, None]).astype(I32), axis=1),
                          N_EXPERTS - 1).astype(I32)
    xs = _dispatch_call(pad_end.astype(I32), (padded - counts).astype(I32), dest_t, hf, n_slots)
    n_live = (pad_end[N_EXPERTS - 1:] // MOE_TILE).astype(I32)
    ys = _expert_call(block_e, n_live, xs, w)
    return _combine_call(dest_t, rg, x2, g_final, ys, final=final)


def kernel(x_prompt, x_sample, g_attn, w_in, g_cq, w_uq, g_ckv, w_uk, w_uv, lam_q1, lam_k1,
           lam_q2, lam_k2, g_subln, w_out, g_ffn, w_router, b_router, w_gate, b_gate, w_up, b_up,
           w_down, b_down, g_final):
    depth = w_in.shape[0]
    gfin = g_final.reshape(1, D_MODEL)

    def trunk(x):
        B, S, D = x.shape
        assert D == D_MODEL and S % TOK_BLOCK == 0
        x2d = x.reshape(B * S, D)
        for l in range(depth):
            w = _prep_layer(l, S, g_attn, w_in, g_cq, w_uq, g_ckv, w_uk, w_uv, lam_q1, lam_k1,
                            lam_q2, lam_k2, g_subln, w_out, g_ffn, w_router, b_router, w_gate,
                            b_gate, w_up, b_up, w_down, b_down)
            x2d = _layer(x2d, B, S, w, gfin, final=(l == depth - 1))
        return x2d.reshape(B, S, D)

    return (trunk(x_prompt), trunk(x_sample))
```

```python
import functools
import math

import jax
import jax.numpy as jnp
from jax import lax
from jax.experimental import pallas as pl
from jax.experimental.pallas import tpu as pltpu

F32 = jnp.float32
BF16 = jnp.bfloat16
I32 = jnp.int32

D_MODEL = 1024
MLA_HEADS = 8
MLA_NOPE = 64
MLA_ROPE = 32
MLA_V = 64
Q_LORA = 384
KV_LORA = 256
MLA_THETA = 10000.0
DIFF_HEADS = 4
DIFF_HD = 64
DIFF_ROT = DIFF_HD // 4
DIFF_THETA = 500000.0
DIFF_QK = DIFF_HEADS * 2 * DIFF_HD
DIFF_VW = DIFF_HEADS * 2 * DIFF_HD
N_EXPERTS = 32
TOP_K = 4
D_FF = 1024
SWIGLU_ALPHA = 1.702
SWIGLU_LIMIT = 7.0
RMS_EPS = 1e-5

LANES = 128
SUBLANES = 8
HEAD_PAD = 128
P_IN_PAD = Q_LORA + KV_LORA + LANES + 2 * DIFF_QK
OFF_CKV = Q_LORA
OFF_KPE = Q_LORA + KV_LORA
OFF_DQ = OFF_KPE + LANES
OFF_DK = OFF_DQ + DIFF_QK
LOG2_E = math.log2(math.e)
MLA_SCALE = LOG2_E / math.sqrt(MLA_NOPE + MLA_ROPE)
DIFF_SCALE = LOG2_E / math.sqrt(DIFF_HD)

TOK_BLOCK = 512
Q_TILE = 1024
Q_SUB = 256
KEY_CHUNK = 256
SUB_INTERLEAVE = 1
MOE_TILE = 512
POST_GROUPS = 1
ROW_TILE = D_MODEL // LANES
DISPATCH_BLOCK = 1024
COMBINE_BLOCK = 256
SLOT_WINDOW = 1024
ISSUE_UNROLL = 8
DMA_THREADS = 2
WAIT_ROWS = 128
SCORE_LOOKAHEAD = 8
ONES_ROWS = 16
VMEM_LIMIT = 56 * 1024 * 1024


def _rms(x, g):
    ms = jnp.mean(x * x, axis=-1, keepdims=True)
    return x * lax.rsqrt(ms + RMS_EPS) * g


def _cparams(sem):
    return pltpu.CompilerParams(dimension_semantics=sem, vmem_limit_bytes=VMEM_LIMIT)


def _proj_kernel(x_ref, ga_ref, win_ref, wdv_ref, gcq_ref, wuq_ref, gckv_ref, wuk_ref, wuv_ref,
                 cm_ref, sm_ref, cd_ref, sd_ref,
                 q_ref, k_ref, vt_ref, dq_ref, dk_ref, dvt_ref):
    x = x_ref[...]
    tm = x.shape[0]
    h = _rms(x, ga_ref[...]).astype(BF16)
    proj = jnp.dot(h, win_ref[...], preferred_element_type=F32)
    ckv = _rms(proj[:, OFF_CKV:OFF_KPE], gckv_ref[...]).astype(BF16)

    lane = lax.broadcasted_iota(I32, (tm, LANES), 1)
    first_m = (lane & (MLA_ROPE // 2)) == 0
    first_d = (lane & (DIFF_ROT // 2)) == 0
    cm, sm, cd, sd = cm_ref[...], sm_ref[...], cd_ref[...], sd_ref[...]

    def rope_m(v):
        partner = jnp.where(first_m, pltpu.roll(v, LANES - MLA_ROPE // 2, 1),
                            pltpu.roll(v, MLA_ROPE // 2, 1))
        return v * cm + partner * sm

    def rope_d(v):
        partner = jnp.where(first_d, pltpu.roll(v, LANES - DIFF_ROT // 2, 1),
                            pltpu.roll(v, DIFF_ROT // 2, 1))
        return v * cd + partner * sd

    cq = _rms(proj[:, 0:Q_LORA], gcq_ref[...]).astype(BF16)
    q = jnp.dot(cq, wuq_ref[...], preferred_element_type=F32)
    kn = jnp.dot(ckv, wuk_ref[...], preferred_element_type=F32)
    kpe = rope_m(proj[:, OFF_KPE:OFF_DQ])
    for hh in range(MLA_HEADS):
        sl = slice(HEAD_PAD * hh, HEAD_PAD * (hh + 1))
        q_ref[:, sl] = (rope_m(q[:, sl]) * MLA_SCALE).astype(BF16)
        k_ref[:, sl] = (kn[:, sl] + kpe).astype(BF16)

    for i in range(DIFF_QK // LANES):
        sl = slice(LANES * i, LANES * (i + 1))
        dq_ref[:, sl] = (rope_d(proj[:, OFF_DQ + LANES * i:OFF_DQ + LANES * (i + 1)])
                         * DIFF_SCALE).astype(BF16)
        dk_ref[:, sl] = rope_d(proj[:, OFF_DK + LANES * i:OFF_DK + LANES * (i + 1)]).astype(BF16)
    vt_ref[0] = jnp.dot(ckv, wuv_ref[...], preferred_element_type=F32).T.astype(BF16)
    dvt_ref[0] = jnp.dot(h, wdv_ref[...], preferred_element_type=F32).T.astype(BF16)


def _proj_call(x2d, S, w):
    T = x2d.shape[0]
    tm = TOK_BLOCK
    nblk = T // tm
    spb = S // tm
    full = lambda shp: pl.BlockSpec(shp, lambda i: (0,) * len(shp))
    tab = pl.BlockSpec((tm, LANES), lambda i: (i % spb, 0))
    tokb = lambda n: pl.BlockSpec((tm, n), lambda i: (i, 0))
    vtb = pl.BlockSpec((1, DIFF_VW, tm), lambda i: (i, 0, 0))
    return pl.pallas_call(
        _proj_kernel,
        grid=(nblk,),
        in_specs=[tokb(D_MODEL), full((1, D_MODEL)), full((D_MODEL, P_IN_PAD)),
                  full((D_MODEL, DIFF_VW)),
                  full((1, Q_LORA)), full((Q_LORA, MLA_HEADS * HEAD_PAD)),
                  full((1, KV_LORA)), full((KV_LORA, MLA_HEADS * HEAD_PAD)),
                  full((KV_LORA, MLA_HEADS * MLA_V)), tab, tab, tab, tab],
        out_specs=[tokb(MLA_HEADS * HEAD_PAD), tokb(MLA_HEADS * HEAD_PAD), vtb,
                   tokb(DIFF_QK), tokb(DIFF_QK), vtb],
        out_shape=[jax.ShapeDtypeStruct((T, MLA_HEADS * HEAD_PAD), BF16),
                   jax.ShapeDtypeStruct((T, MLA_HEADS * HEAD_PAD), BF16),
                   jax.ShapeDtypeStruct((nblk, MLA_HEADS * MLA_V, tm), BF16),
                   jax.ShapeDtypeStruct((T, DIFF_QK), BF16),
                   jax.ShapeDtypeStruct((T, DIFF_QK), BF16),
                   jax.ShapeDtypeStruct((nblk, DIFF_VW, tm), BF16)],
        compiler_params=_cparams(("arbitrary",)),
        name="proj",
    )(x2d, w["g_attn"], w["w_in"], w["w_dv"], w["g_cq"], w["w_uq"], w["g_ckv"], w["w_uk"],
      w["w_uv"], w["cos_m"], w["sin_m"], w["cos_d"], w["sin_d"])


def _attn_kernel(*refs, mla, lambda_init):
    if mla:
        q_ref, k_ref, vt_ref, o_ref = refs
    else:
        lam_ref, g_ref, q_ref, k_ref, vt_ref, o_ref = refs
    dv = MLA_V if mla else 2 * DIFF_HD
    kc = KEY_CHUNK
    vt_blk = vt_ref.shape[2]
    n_chunks = k_ref.shape[0] // kc
    ones = jnp.ones((ONES_ROWS, kc), BF16)
    if not mla:
        a = lam_ref[...]
        s1 = jnp.sum(a[0:1, :] * a[1:2, :], axis=1, keepdims=True)
        s2 = jnp.sum(a[2:3, :] * a[3:4, :], axis=1, keepdims=True)
        lam = jnp.exp(s1) - jnp.exp(s2) + lambda_init

    def query(t, m):
        rows = slice(Q_SUB * t, Q_SUB * (t + 1))
        if mla:
            return q_ref[rows, HEAD_PAD * m:HEAD_PAD * (m + 1)]
        lane = lax.broadcasted_iota(I32, (Q_SUB, LANES), 1)
        keep = (lane < DIFF_HD) if m == 0 else (lane >= DIFF_HD)
        qf = q_ref[rows, :]
        return jnp.where(keep, qf, jnp.zeros_like(qf))

    def scores(t, c, m):
        if mla:
            kb = k_ref[kc * c:kc * (c + 1), HEAD_PAD * m:HEAD_PAD * (m + 1)]
        else:
            kb = k_ref[kc * c:kc * (c + 1), :]
        return lax.dot_general(kb, qs[t][m], (((1,), (1,)), ((), ())),
                               preferred_element_type=F32)

    def finish(t, acc):
        outs = [acc[m][0:dv] / acc[m][dv:dv + 1] for m in range(2)]
        if mla:
            o = jnp.concatenate(outs, axis=0)
        else:
            o = outs[0] - lam * outs[1]
            ms = jnp.mean(o * o, axis=0, keepdims=True)
            o = o * lax.rsqrt(ms + RMS_EPS) * g_ref[...] * (1.0 - lambda_init)
        o_ref[Q_SUB * t:Q_SUB * (t + 1), :] = o.T.astype(BF16)

    n_sub = q_ref.shape[0] // Q_SUB
    qs = [[query(t, m) for m in range(2)] for t in range(n_sub)]
    order = [(t, c, m) for tp in range(0, n_sub, SUB_INTERLEAVE) for c in range(n_chunks)
             for t in range(tp, tp + SUB_INTERLEAVE) for m in range(2)]
    pending = [scores(*order[i]) for i in range(min(SCORE_LOOKAHEAD, len(order)))]
    m_all = [[jnp.full((1, Q_SUB), -jnp.inf, F32) for _ in range(2)] for _ in range(n_sub)]
    acc_all = [[jnp.zeros((dv + ONES_ROWS, Q_SUB), F32) for _ in range(2)] for _ in range(n_sub)]
    for i, (t, c, m) in enumerate(order):
        m_i, acc = m_all[t], acc_all[t]
        s = pending.pop(0)
        if i + SCORE_LOOKAHEAD < len(order):
            pending.append(scores(*order[i + SCORE_LOOKAHEAD]))
        vrows = slice(MLA_V * m, MLA_V * (m + 1)) if mla else slice(0, dv)
        vcols = slice((kc * c) % vt_blk, (kc * c) % vt_blk + kc)
        vb = jnp.concatenate([vt_ref[(kc * c) // vt_blk, vrows, vcols], ones], axis=0)
        m_new = jnp.maximum(m_i[m], jnp.max(s, axis=0, keepdims=True))
        alpha = jnp.exp2(m_i[m] - m_new)
        p = jnp.exp2(s - m_new).astype(BF16)
        acc[m] = alpha * acc[m] + jnp.dot(vb, p, preferred_element_type=F32)
        m_i[m] = m_new
        if c == n_chunks - 1 and m == 1:
            finish(t, acc)


def _attn_call(q, k, vt, B, S, *, mla, lam=None, g=None, lambda_init=0.0):
    T = q.shape[0]
    tq = min(Q_TILE, S)
    nq = S // tq
    vt_blk = vt.shape[2]
    assert vt_blk % KEY_CHUNK == 0 and S % vt_blk == 0
    qw = 2 * HEAD_PAD if mla else LANES
    in_specs = [pl.BlockSpec((tq, qw), lambda b, j, i: (b * nq + i, j)),
                pl.BlockSpec((S, qw), lambda b, j, i: (b, j)),
                pl.BlockSpec((S // vt_blk, LANES, vt_blk), lambda b, j, i: (b, j, 0))]
    args = [q, k, vt]
    if not mla:
        in_specs = [pl.BlockSpec((SUBLANES, LANES), lambda b, j, i: (0, 0)),
                    pl.BlockSpec((2 * DIFF_HD, 1), lambda b, j, i: (0, 0))] + in_specs
        args = [lam, g] + args
    return pl.pallas_call(
        functools.partial(_attn_kernel, mla=mla, lambda_init=lambda_init),
        grid=(B, 4, nq),
        in_specs=in_specs,
        out_specs=pl.BlockSpec((tq, LANES), lambda b, j, i: (b * nq + i, j)),
        out_shape=jax.ShapeDtypeStruct((T, 4 * LANES), BF16),
        compiler_params=_cparams(("arbitrary", "arbitrary", "arbitrary")),
        name="attn_mla" if mla else "attn_diff",
    )(*args)


def _post_kernel(x_ref, om_ref, od_ref, woa_ref, wob_ref, gf_ref, wrh_ref, wrl_ref, br_ref,
                 x2_ref, hf_ref, ri_ref, rg_ref, cnt_ref, base_ref):
    step = pl.program_id(0)

    @pl.when(step == 0)
    def _():
        base_ref[...] = jnp.zeros_like(base_ref)

    n = x_ref.shape[0] // POST_GROUPS
    logits = []
    for gi in range(POST_GROUPS):
        rows = slice(gi * n, (gi + 1) * n)
        x2 = (x_ref[rows, :]
              + jnp.dot(om_ref[rows, :], woa_ref[...], preferred_element_type=F32)
              + jnp.dot(od_ref[rows, :], wob_ref[...], preferred_element_type=F32))
        x2_ref[rows, :] = x2
        hf = _rms(x2, gf_ref[...])
        _store_rows(hf_ref.at[pl.ds(gi * n * ROW_TILE, n * ROW_TILE), :], hf)
        h_hi = hf.astype(BF16)
        h_lo = (hf - h_hi.astype(F32)).astype(BF16)
        logits.append(jnp.dot(h_hi, wrh_ref[...], preferred_element_type=F32)
                      + jnp.dot(h_lo, wrh_ref[...], preferred_element_type=F32)
                      + jnp.dot(h_hi, wrl_ref[...], preferred_element_type=F32)
                      + br_ref[...])

    lane = lax.broadcasted_iota(I32, (n, LANES), 1).astype(F32)
    row = lax.broadcasted_iota(I32, (n, n), 0)
    col = lax.broadcasted_iota(I32, (n, n), 1)
    lower = jnp.where(row > col, 1.0, 0.0).astype(BF16)
    base = base_ref[0:1, :]
    for gi in range(POST_GROUPS):
        rows = slice(gi * n, (gi + 1) * n)
        work = logits[gi]
        sels, vals, idxs = [], [], []
        for _ in range(TOP_K):
            mx = jnp.max(work, axis=1, keepdims=True)
            idx = jnp.min(jnp.where(work == mx, lane, float(LANES)), axis=1, keepdims=True)
            sel = lane == idx
            work = jnp.where(sel, -jnp.inf, work)
            sels.append(sel)
            vals.append(mx)
            idxs.append(idx)
        es = [jnp.exp(v - vals[0]) for v in vals]
        den = es[0] + es[1] + es[2] + es[3]
        gates = [e / den for e in es]

        multi = jnp.zeros((n, LANES), F32)
        for sel in sels:
            multi = multi + jnp.where(sel, 1.0, 0.0)
        prefix = jnp.dot(lower, multi.astype(BF16), preferred_element_type=F32) + base
        ri = jnp.zeros((n, LANES), F32)
        rg = jnp.zeros((n, LANES), F32)
        for kk in range(TOP_K):
            rank = jnp.sum(jnp.where(sels[kk], prefix, 0.0), axis=1, keepdims=True)
            ri = jnp.where(lane == float(kk), idxs[kk], ri)
            ri = jnp.where(lane == float(TOP_K + kk), rank, ri)
            rg = jnp.where(lane == float(kk), gates[kk], rg)
        ri_ref[rows, :] = ri.astype(I32)
        rg_ref[rows, :] = rg
        base = base + jnp.sum(multi, axis=0, keepdims=True)
    base_ref[0:1, :] = base
    cnt_ref[...] = jnp.broadcast_to(base, cnt_ref.shape)


def _post_call(x2d, o_mla, o_diff, w):
    T = x2d.shape[0]
    tm = TOK_BLOCK
    full = lambda shp: pl.BlockSpec(shp, lambda i: (0,) * len(shp))
    tokb = lambda n: pl.BlockSpec((tm, n), lambda i: (i, 0))
    return pl.pallas_call(
        _post_kernel,
        grid=(T // tm,),
        in_specs=[tokb(D_MODEL), tokb(4 * LANES), tokb(4 * LANES),
                  full((4 * LANES, D_MODEL)), full((4 * LANES, D_MODEL)), full((1, D_MODEL)),
                  full((D_MODEL, LANES)), full((D_MODEL, LANES)), full((1, LANES))],
        out_specs=[tokb(D_MODEL), pl.BlockSpec((tm * ROW_TILE, LANES), lambda i: (i, 0)),
                   tokb(LANES), tokb(LANES), full((SUBLANES, LANES))],
        out_shape=[jax.ShapeDtypeStruct((T, D_MODEL), F32),
                   jax.ShapeDtypeStruct((T * ROW_TILE, LANES), F32),
                   jax.ShapeDtypeStruct((T, LANES), I32),
                   jax.ShapeDtypeStruct((T, LANES), F32),
                   jax.ShapeDtypeStruct((SUBLANES, LANES), F32)],
        scratch_shapes=[pltpu.VMEM((SUBLANES, LANES), F32)],
        compiler_params=_cparams(("arbitrary",)),
        name="post",
    )(x2d, o_mla, o_diff, w["w_out_a"], w["w_out_b"], w["g_ffn"],
      w["w_router_hi"], w["w_router_lo"], w["b_router"])


def _store_rows(ref, val):
    n = val.shape[0]
    for s in range(ROW_TILE):
        ref[pl.ds(s, n, stride=ROW_TILE), :] = val[:, LANES * s:LANES * (s + 1)]


def _load_rows(ref, n):
    return jnp.concatenate([ref[pl.ds(s, n, stride=ROW_TILE), :] for s in range(ROW_TILE)],
                           axis=1)


def _row_copy(src, si, dst, di, sem):
    return pltpu.make_async_copy(
        src.at[pl.ds(pl.multiple_of(si * ROW_TILE, ROW_TILE), ROW_TILE), :],
        dst.at[pl.ds(pl.multiple_of(di * ROW_TILE, ROW_TILE), ROW_TILE), :], sem)


def _wait_rows(src, dst, n_rows, sem):
    for _ in range(0, n_rows, WAIT_ROWS):
        pltpu.make_async_copy(src.at[pl.ds(0, WAIT_ROWS * ROW_TILE), :],
                              dst.at[pl.ds(0, WAIT_ROWS * ROW_TILE), :], sem).wait()


def _dispatch_kernel(pend_ref, npad_ref, *refs):
    dest_refs = refs[:TOP_K]
    hf_ref, xs_ref, zero_ref, sem, zsem = refs[TOP_K:]
    tb = dest_refs[0].shape[0]

    @pl.when(pl.program_id(0) == 0)
    def _():
        zero_ref[...] = jnp.zeros_like(zero_ref)

        def zcopy(e):
            start = pl.multiple_of((pend_ref[e] - MOE_TILE) * ROW_TILE, MOE_TILE * ROW_TILE)
            return pltpu.make_async_copy(
                zero_ref, xs_ref.at[pl.ds(start, MOE_TILE * ROW_TILE), :], zsem)

        def zstart(e, c):
            @pl.when(npad_ref[e] > 0)
            def _():
                zcopy(e).start()
            return c

        def zwait(e, c):
            @pl.when(npad_ref[e] > 0)
            def _():
                zcopy(e).wait()
            return c

        lax.fori_loop(0, N_EXPERTS, zstart, 0)
        lax.fori_loop(0, N_EXPERTS, zwait, 0)

        def tcopy(j):
            start = pl.multiple_of(j * (MOE_TILE * ROW_TILE), MOE_TILE * ROW_TILE)
            return pltpu.make_async_copy(
                zero_ref, xs_ref.at[pl.ds(start, MOE_TILE * ROW_TILE), :], zsem)

        def tstart(j, c):
            tcopy(j).start()
            return c

        def twait(j, c):
            tcopy(j).wait()
            return c

        first_free = pend_ref[N_EXPERTS - 1] // MOE_TILE
        n_blocks = xs_ref.shape[0] // (MOE_TILE * ROW_TILE)
        lax.fori_loop(first_free, n_blocks, tstart, 0)
        lax.fori_loop(first_free, n_blocks, twait, 0)

    def issue(j, c):
        base = pl.multiple_of(j * ISSUE_UNROLL, ISSUE_UNROLL)
        for u in range(ISSUE_UNROLL):
            for kk in range(TOP_K):
                _row_copy(hf_ref, base + u, xs_ref, dest_refs[kk][base + u],
                          sem).start(priority=kk % DMA_THREADS)
        return c

    lax.fori_loop(0, tb // ISSUE_UNROLL, issue, 0)
    _wait_rows(hf_ref, xs_ref, TOP_K * tb, sem)


def _dispatch_call(pad_end, n_pad, dest_t, hf, n_slots):
    T = hf.shape[0] // ROW_TILE
    tb = DISPATCH_BLOCK
    return pl.pallas_call(
        _dispatch_kernel,
        grid_spec=pltpu.PrefetchScalarGridSpec(
            num_scalar_prefetch=2,
            grid=(T // tb,),
            in_specs=[pl.BlockSpec((tb,), lambda i, pe, npd, k=k: (k * (T // tb) + i,),
                                   memory_space=pltpu.SMEM) for k in range(TOP_K)]
            + [pl.BlockSpec((tb * ROW_TILE, LANES), lambda i, pe, npd: (i, 0))],
            out_specs=pl.BlockSpec(memory_space=pl.ANY),
            scratch_shapes=[pltpu.VMEM((MOE_TILE * ROW_TILE, LANES), F32),
                            pltpu.SemaphoreType.DMA, pltpu.SemaphoreType.DMA]),
        out_shape=jax.ShapeDtypeStruct((n_slots * ROW_TILE, LANES), F32),
        compiler_params=_cparams(("arbitrary",)),
        name="dispatch",
    )(pad_end, n_pad, *([dest_t] * TOP_K), hf)


def _expert_kernel(be_ref, nlive_ref, xs_ref, wg_ref, bg_ref, wu_ref, bu_ref, wd_ref, bd_ref,
                   ys_ref, wgb, wub, wdb):
    i = pl.program_id(0)
    prev = be_ref[jnp.maximum(i - 1, 0)]

    @pl.when(jnp.logical_or(i == 0, be_ref[i] != prev))
    def _():
        wgb[...] = wg_ref[0].astype(BF16)
        wub[...] = wu_ref[0].astype(BF16)
        wdb[...] = wd_ref[0].astype(BF16)

    live = i < nlive_ref[0]

    @pl.when(live)
    def _():
        x = _load_rows(xs_ref, MOE_TILE).astype(BF16)
        g = jnp.dot(x, wgb[...], preferred_element_type=F32) + bg_ref[0]
        u = jnp.dot(x, wub[...], preferred_element_type=F32) + bu_ref[0]
        g = jnp.minimum(g, SWIGLU_LIMIT)
        u = jnp.clip(u, -SWIGLU_LIMIT, SWIGLU_LIMIT)
        a = g * (1.0 / (1.0 + jnp.exp(-SWIGLU_ALPHA * g))) * (u + 1.0)
        y = jnp.dot(a.astype(BF16), wdb[...], preferred_element_type=F32) + bd_ref[0]
        _store_rows(ys_ref, y)

    @pl.when(jnp.logical_not(live))
    def _():
        ys_ref[...] = jnp.zeros_like(ys_ref)


def _expert_call(block_e, n_live, xs, w):
    n_slots = xs.shape[0] // ROW_TILE
    wspec = lambda r, c: pl.BlockSpec((1, r, c), lambda i, be, nl: (be[i], 0, 0))
    slotb = pl.BlockSpec((MOE_TILE * ROW_TILE, LANES), lambda i, be, nl: (i, 0))
    return pl.pallas_call(
        _expert_kernel,
        grid_spec=pltpu.PrefetchScalarGridSpec(
            num_scalar_prefetch=2,
            grid=(n_slots // MOE_TILE,),
            in_specs=[slotb, wspec(D_MODEL, D_FF), wspec(1, D_FF), wspec(D_MODEL, D_FF),
                      wspec(1, D_FF), wspec(D_FF, D_MODEL), wspec(1, D_MODEL)],
            out_specs=slotb,
            scratch_shapes=[pltpu.VMEM((D_MODEL, D_FF), BF16), pltpu.VMEM((D_MODEL, D_FF), BF16),
                            pltpu.VMEM((D_FF, D_MODEL), BF16)]),
        out_shape=jax.ShapeDtypeStruct((n_slots * ROW_TILE, LANES), F32),
        compiler_params=_cparams(("arbitrary",)),
        name="experts",
    )(block_e, n_live, xs, w["w_gate"], w["b_gate"], w["w_up"], w["b_up"], w["w_down"],
      w["b_down"])


def _combine_kernel(*refs, final):
    cur_refs, next_refs = refs[:TOP_K], refs[TOP_K:2 * TOP_K]
    rg_ref, x2_ref, gfin_ref, ys_ref, out_ref, buf, sem = refs[2 * TOP_K:]
    tb = x2_ref.shape[0]
    per = cur_refs[0].shape[0] // tb
    step = pl.program_id(0)
    slot = lax.rem(step, 2)

    def gather(d_refs, blk, s):
        off = lax.rem(blk, per) * tb

        def issue(j, c):
            base = pl.multiple_of(j * ISSUE_UNROLL, ISSUE_UNROLL)
            for u in range(ISSUE_UNROLL):
                for kk in range(TOP_K):
                    _row_copy(ys_ref, d_refs[kk][off + base + u], buf.at[s, kk], base + u,
                              sem.at[s]).start(priority=kk % DMA_THREADS)
            return c

        lax.fori_loop(0, tb // ISSUE_UNROLL, issue, 0)

    @pl.when(step == 0)
    def _():
        gather(cur_refs, step, slot)

    @pl.when(step + 1 < pl.num_programs(0))
    def _():
        gather(next_refs, step + 1, 1 - slot)

    _wait_rows(ys_ref, buf.at[slot, 0], TOP_K * tb, sem.at[slot])

    rg = rg_ref[...]
    y = x2_ref[...]
    for kk in range(TOP_K):
        y = y + _load_rows(buf.at[slot, kk], tb) * rg[:, kk:kk + 1]
    if final:
        y = _rms(y, gfin_ref[...])
    out_ref[...] = y


def _combine_call(dest_t, rg, x2, g_final, ys, *, final):
    T = x2.shape[0]
    tb = COMBINE_BLOCK
    last = T // tb - 1
    per = SLOT_WINDOW // tb
    nwin = T // SLOT_WINDOW
    cur = [pl.BlockSpec((SLOT_WINDOW,), lambda i, k=k: (k * nwin + i // per,),
                        memory_space=pltpu.SMEM) for k in range(TOP_K)]
    nxt = [pl.BlockSpec((SLOT_WINDOW,),
                        lambda i, k=k: (k * nwin + jnp.minimum(i + 1, last) // per,),
                        memory_space=pltpu.SMEM) for k in range(TOP_K)]
    return pl.pallas_call(
        functools.partial(_combine_kernel, final=final),
        grid=(T // tb,),
        in_specs=cur + nxt + [
                  pl.BlockSpec((tb, LANES), lambda i: (i, 0)),
                  pl.BlockSpec((tb, D_MODEL), lambda i: (i, 0)),
                  pl.BlockSpec((1, D_MODEL), lambda i: (0, 0)),
                  pl.BlockSpec(memory_space=pl.ANY)],
        out_specs=pl.BlockSpec((tb, D_MODEL), lambda i: (i, 0)),
        out_shape=jax.ShapeDtypeStruct((T, D_MODEL), F32),
        scratch_shapes=[pltpu.VMEM((2, TOP_K, tb * ROW_TILE, LANES), F32),
                        pltpu.SemaphoreType.DMA((2,))],
        compiler_params=_cparams(("arbitrary",)),
        name="combine",
    )(*([dest_t] * (2 * TOP_K)), rg, x2, g_final, ys)


def _rope_tables(S, rot_dim, theta, group, lead):
    half = rot_dim // 2
    inv_freq = jnp.power(theta, -2.0 * jnp.arange(half, dtype=F32) / rot_dim)
    ang = jnp.arange(S, dtype=F32)[:, None] * inv_freq[None, :]
    cos = jnp.cos(ang)
    sin = jnp.sin(ang)
    rest = group - lead - rot_dim
    cg = jnp.concatenate([jnp.ones((S, lead), F32), cos, cos, jnp.ones((S, rest), F32)], axis=1)
    sg = jnp.concatenate([jnp.zeros((S, lead), F32), -sin, sin, jnp.zeros((S, rest), F32)], axis=1)
    reps = LANES // group
    return jnp.tile(cg, (1, reps)), jnp.tile(sg, (1, reps))


def _prep_layer(l, S, g_attn, w_in, g_cq, w_uq, g_ckv, w_uk, w_uv, lam_q1, lam_k1, lam_q2,
                lam_k2, g_subln, w_out, g_ffn, w_router, b_router, w_gate, b_gate, w_up, b_up,
                w_down, b_down):
    w = {}
    wi = w_in[l]
    kpe_cols = jnp.zeros((D_MODEL, LANES), F32).at[:, MLA_NOPE:MLA_NOPE + MLA_ROPE].set(
        wi[:, OFF_KPE:OFF_KPE + MLA_ROPE])
    dv0 = OFF_KPE + MLA_ROPE + 2 * DIFF_QK
    w["w_in"] = jnp.concatenate(
        [wi[:, :OFF_KPE], kpe_cols, wi[:, OFF_KPE + MLA_ROPE:dv0]], axis=1).astype(BF16)
    w["w_dv"] = wi[:, dv0:].astype(BF16)
    dqk = MLA_NOPE + MLA_ROPE
    wq = w_uq[l].reshape(Q_LORA, MLA_HEADS, dqk)
    w["w_uq"] = jnp.pad(wq, ((0, 0), (0, 0), (0, HEAD_PAD - dqk))).reshape(
        Q_LORA, MLA_HEADS * HEAD_PAD).astype(BF16)
    wk = w_uk[l].reshape(KV_LORA, MLA_HEADS, MLA_NOPE)
    w["w_uk"] = jnp.pad(wk, ((0, 0), (0, 0), (0, HEAD_PAD - MLA_NOPE))).reshape(
        KV_LORA, MLA_HEADS * HEAD_PAD).astype(BF16)
    w["w_uv"] = w_uv[l].astype(BF16)
    w["g_attn"] = g_attn[l].reshape(1, D_MODEL)
    w["g_cq"] = g_cq[l].reshape(1, Q_LORA)
    w["g_ckv"] = g_ckv[l].reshape(1, KV_LORA)
    w["cos_m"], w["sin_m"] = _rope_tables(S, MLA_ROPE, MLA_THETA, LANES, MLA_NOPE)
    w["cos_d"], w["sin_d"] = _rope_tables(S, DIFF_ROT, DIFF_THETA, DIFF_HD, 0)
    lam = jnp.stack([lam_q1[l], lam_k1[l], lam_q2[l], lam_k2[l]]).astype(F32)
    w["lam"] = jnp.pad(lam, ((0, SUBLANES - lam.shape[0]), (0, LANES - DIFF_HD)))
    w["g_subln"] = g_subln[l].reshape(2 * DIFF_HD, 1)
    wo = w_out[l].astype(BF16)
    w["w_out_a"] = wo[:MLA_HEADS * MLA_V]
    w["w_out_b"] = wo[MLA_HEADS * MLA_V:]
    w["g_ffn"] = g_ffn[l].reshape(1, D_MODEL)
    wr = jnp.pad(w_router[l], ((0, 0), (0, LANES - N_EXPERTS)))
    w["w_router_hi"] = wr.astype(BF16)
    w["w_router_lo"] = (wr - w["w_router_hi"].astype(F32)).astype(BF16)
    w["b_router"] = jnp.pad(b_router[l].astype(F32), (0, LANES - N_EXPERTS),
                            constant_values=-jnp.inf).reshape(1, LANES)
    w["w_gate"], w["w_up"], w["w_down"] = w_gate[l], w_up[l], w_down[l]
    w["b_gate"] = b_gate[l].reshape(N_EXPERTS, 1, D_FF)
    w["b_up"] = b_up[l].reshape(N_EXPERTS, 1, D_FF)
    w["b_down"] = b_down[l].reshape(N_EXPERTS, 1, D_MODEL)
    w["lambda_init"] = 0.8 - 0.6 * math.exp(-0.3 * l)
    return w


def _layer(x2d, B, S, w, g_final, *, final):
    T = x2d.shape[0]
    q, k, vt, dq, dk, dvt = _proj_call(x2d, S, w)
    o_mla = _attn_call(q, k, vt, B, S, mla=True)
    o_diff = _attn_call(dq, dk, dvt, B, S, mla=False, lam=w["lam"], g=w["g_subln"],
                        lambda_init=w["lambda_init"])
    x2, hf, ri, rg, cnt = _post_call(x2d, o_mla, o_diff, w)

    counts = cnt[0, :N_EXPERTS].astype(I32)
    padded = ((counts + MOE_TILE - 1) // MOE_TILE) * MOE_TILE
    pad_end = jnp.cumsum(padded)
    pad_start = pad_end - padded
    idx_t = ri[:, :TOP_K].T
    dest_t = (pad_start[idx_t] + ri[:, TOP_K:2 * TOP_K].T).reshape(-1)
    n_blocks = (T * TOP_K) // MOE_TILE + N_EXPERTS
    n_slots = n_blocks * MOE_TILE
    blk_start = jnp.arange(n_blocks, dtype=I32) * MOE_TILE
    block_e = jnp.minimum(jnp.sum((pad_end[None, :] <= blk_start[:, None]).astype(I32), axis=1),
                          N_EXPERTS - 1).astype(I32)
    xs = _dispatch_call(pad_end.astype(I32), (padded - counts).astype(I32), dest_t, hf, n_slots)
    n_live = (pad_end[N_EXPERTS - 1:] // MOE_TILE).astype(I32)
    ys = _expert_call(block_e, n_live, xs, w)
    return _combine_call(dest_t, rg, x2, g_final, ys, final=final)


def kernel(x_prompt, x_sample, g_attn, w_in, g_cq, w_uq, g_ckv, w_uk, w_uv, lam_q1, lam_k1,
           lam_q2, lam_k2, g_subln, w_out, g_ffn, w_router, b_router, w_gate, b_gate, w_up, b_up,
           w_down, b_down, g_final):
    depth = w_in.shape[0]
    gfin = g_final.reshape(1, D_MODEL)

    def trunk(x):
        B, S, D = x.shape
        assert D == D_MODEL and S % TOK_BLOCK == 0
        x2d = x.reshape(B * S, D)
        for l in range(depth):
            w = _prep_layer(l, S, g_attn, w_in, g_cq, w_uq, g_ckv, w_uk, w_uv, lam_q1, lam_k1,
                            lam_q2, lam_k2, g_subln, w_out, g_ffn, w_router, b_router, w_gate,
                            b_gate, w_up, b_up, w_down, b_down)
            x2d = _layer(x2d, B, S, w, gfin, final=(l == depth - 1))
        return x2d.reshape(B, S, D)

    return (trunk(x_prompt), trunk(x_sample))
```
